```python
import math
import jax, jax.numpy as jnp
from jax import lax
import numpy as np

D_MODEL = 1024
BATCH = 8
SEQ = 2048
DEPTH = 2
DEC_BATCH = 128
DEC_SEQ = 4
PAST_LEN = 8192
PAGE_SIZE = 128

HEAD_DIM = 64
N_HEADS = 16
KV_HEADS_A = 2
KV_HEADS_B = 4
WINDOW = 128
BLOCK = 128
IDX_HEADS = 8
IDX_DIM = 64
TOPK_MAX = 256
N_GROUPS = 4
EXPERTS_PER_GROUP = 8
N_EXPERTS = N_GROUPS * EXPERTS_PER_GROUP
TOP_K_IN_GROUP = 2
D_FF_EXPERT = 256
ROPE_THETA = 10000.0
LN_EPS = 1e-5
DEEPNORM_ALPHA = (2 * DEPTH) ** 0.25
DEEPNORM_BETA = (8 * DEPTH) ** -0.25
N_A_LAYERS = (DEPTH + 1) // 2
N_B_LAYERS = DEPTH // 2

kernel_name = "hybrid_swa_sink_dsa_hmoe_deepnorm_step"


def layer_norm(x, g, b):
    xf = x.astype(jnp.float32)
    mu = xf.mean(-1, keepdims=True)
    var = jnp.square(xf - mu).mean(-1, keepdims=True)
    y = (xf - mu) * lax.rsqrt(var + LN_EPS) * g.astype(jnp.float32) + b.astype(jnp.float32)
    return y.astype(x.dtype)


def rope(x, pos):
    d = x.shape[-1]
    half = d // 2
    inv = jnp.exp(jnp.arange(half, dtype=jnp.float32) * (-2.0 * math.log(ROPE_THETA) / d))
    ang = pos.astype(jnp.float32)[:, None] * inv[None, :]
    cos = jnp.cos(ang)[:, None, :]
    sin = jnp.sin(ang)[:, None, :]
    xf = x.astype(jnp.float32)
    x1, x2 = xf[..., :half], xf[..., half:]
    return jnp.concatenate([x1 * cos - x2 * sin, x1 * sin + x2 * cos], axis=-1).astype(x.dtype)


def _offsets(sizes):
    out, acc = [], 0
    for s in sizes[:-1]:
        acc += s
        out.append(acc)
    return out


def sink_attention(q, k, v, mask, sinks):
    s = jnp.einsum('...qkgd,...skd->...kgqs', q.astype(jnp.float32), k.astype(jnp.float32)) * HEAD_DIM ** -0.5
    s = jnp.where(mask, s, -jnp.inf)
    sink = sinks.astype(jnp.float32).reshape(s.shape[-4], s.shape[-3], 1, 1)
    m = jnp.maximum(s.max(-1, keepdims=True), sink)
    p = jnp.exp(s - m)
    p = p / (p.sum(-1, keepdims=True) + jnp.exp(sink - m))
    o = jnp.einsum('...kgqs,...skd->...qkgd', p, v.astype(jnp.float32))
    return o.astype(q.dtype)


def swa_project(x, w_qkv, b_qkv, pos):
    B, L, _ = x.shape
    sizes = [N_HEADS * HEAD_DIM, KV_HEADS_A * HEAD_DIM, KV_HEADS_A * HEAD_DIM]
    q, k, v = jnp.split(x @ w_qkv + b_qkv, _offsets(sizes), axis=-1)
    q = rope(q.reshape(B, L, N_HEADS, HEAD_DIM), pos)
    k = rope(k.reshape(B, L, KV_HEADS_A, HEAD_DIM), pos)
    v = v.reshape(B, L, KV_HEADS_A, HEAD_DIM)
    return q, k, v


def swa_prompt(x, w_qkv, b_qkv, sinks, w_o):
    B, S, _ = x.shape
    G = N_HEADS // KV_HEADS_A
    q, k, v = swa_project(x, w_qkv, b_qkv, jnp.arange(S))
    nb = S // BLOCK
    qb = q.reshape(B, nb, BLOCK, KV_HEADS_A, G, HEAD_DIM)
    kb = k.reshape(B, nb, BLOCK, KV_HEADS_A, HEAD_DIM)
    vb = v.reshape(B, nb, BLOCK, KV_HEADS_A, HEAD_DIM)
    kband = jnp.concatenate([jnp.concatenate([jnp.zeros_like(kb[:, :1]), kb[:, :-1]], 1), kb], 2)
    vband = jnp.concatenate([jnp.concatenate([jnp.zeros_like(vb[:, :1]), vb[:, :-1]], 1), vb], 2)
    qi = jnp.arange(BLOCK)[:, None]
    kj = jnp.arange(2 * BLOCK)[None, :]
    rel = qi - kj + BLOCK
    band = (rel >= 0) & (rel < WINDOW)
    exists = (jnp.arange(nb)[:, None, None] > 0) | (kj[None] >= BLOCK)
    mask = (band[None] & exists)[None, :, None, None]
    o = sink_attention(qb, kband, vband, mask, sinks).reshape(B, S, N_HEADS * HEAD_DIM)
    w = min(WINDOW, S)
    return o @ w_o, k[:, S - w:], v[:, S - w:]


def swa_sample(x, buf_k, buf_v, w_qkv, b_qkv, sinks, w_o):
    Bd, T, _ = x.shape
    G = N_HEADS // KV_HEADS_A
    W = buf_k.shape[1]
    pos = PAST_LEN + jnp.arange(T)
    q, k, v = swa_project(x, w_qkv, b_qkv, pos)
    keys = jnp.concatenate([buf_k, k], axis=1)
    vals = jnp.concatenate([buf_v, v], axis=1)
    kpos = jnp.concatenate([PAST_LEN - W + jnp.arange(W), pos])
    rel = pos[:, None] - kpos[None, :]
    mask = (rel >= 0) & (rel < WINDOW)
    o = sink_attention(q.reshape(Bd, T, KV_HEADS_A, G, HEAD_DIM), keys, vals, mask, sinks)
    o = o.reshape(Bd, T, N_HEADS * HEAD_DIM)
    return o @ w_o, keys[:, -W:], vals[:, -W:]


def dsa_project(x, w_in, pos):
    B, L, _ = x.shape
    sizes = [N_HEADS * HEAD_DIM, KV_HEADS_B * HEAD_DIM, KV_HEADS_B * HEAD_DIM,
             IDX_HEADS * IDX_DIM, IDX_DIM, IDX_HEADS]
    q, k, v, qi, ki, wi = jnp.split(x @ w_in, _offsets(sizes), axis=-1)
    q = rope(q.reshape(B, L, N_HEADS, HEAD_DIM), pos)
    k = rope(k.reshape(B, L, KV_HEADS_B, HEAD_DIM), pos)
    v = v.reshape(B, L, KV_HEADS_B, HEAD_DIM)
    qi = rope(qi.reshape(B, L, IDX_HEADS, IDX_DIM), pos)
    ki = rope(ki.reshape(B, L, 1, IDX_DIM), pos)[:, :, 0]
    wi = wi * IDX_HEADS ** -0.5
    return q, k, v, qi, ki, wi


def indexer_scores(qi, wi, ki):
    dots = jnp.einsum('bqhd,bsd->bqhs', qi.astype(jnp.float32), ki.astype(jnp.float32)) * IDX_DIM ** -0.5
    return jnp.einsum('bqhs,bqh->bqs', jax.nn.relu(dots), wi.astype(jnp.float32))


def gathered_attention(q, kg, vg, valid):
    B, Q = q.shape[:2]
    G = N_HEADS // KV_HEADS_B
    qg = q.reshape(B, Q, KV_HEADS_B, G, HEAD_DIM).astype(jnp.float32)
    s = jnp.einsum('bqkgd,bqskd->bqkgs', qg, kg.astype(jnp.float32)) * HEAD_DIM ** -0.5
    s = jnp.where(valid[:, :, None, None, :], s, -jnp.inf)
    p = jax.nn.softmax(s, axis=-1)
    o = jnp.einsum('bqkgs,bqskd->bqkgd', p, vg.astype(jnp.float32))
    return o.reshape(B, Q, N_HEADS * HEAD_DIM).astype(q.dtype)


_gather_rows = jax.vmap(lambda rows, ix: rows[ix])


def dsa_prompt(x, w_in, w_o):
    B, S, _ = x.shape
    pos = jnp.arange(S)
    q, k, v, qi, ki, wi = dsa_project(x, w_in, pos)
    topk = min(TOPK_MAX, S // 4)
    nb = S // BLOCK

    def to_blocks(a):
        return jnp.moveaxis(a.reshape(B, nb, BLOCK, *a.shape[2:]), 1, 0)

    def block_fn(args):
        qb, qib, wib, tpos = args
        sc = indexer_scores(qib, wib, ki)
        sc = jnp.where(pos[None, None, :] <= tpos[None, :, None], sc, -jnp.inf)
        _, idx = lax.top_k(sc, topk)
        valid = idx <= tpos[None, :, None]
        return gathered_attention(qb, _gather_rows(k, idx), _gather_rows(v, idx), valid)

    o = lax.map(block_fn, (to_blocks(q), to_blocks(qi), to_blocks(wi), pos.reshape(nb, BLOCK)))
    o = jnp.moveaxis(o, 0, 1).reshape(B, S, N_HEADS * HEAD_DIM)
    return o @ w_o, k, v, ki


def dsa_sample(x, pool_k, pool_v, pool_ik, page_table, w_in, w_o):
    Bd, T, _ = x.shape
    pos = PAST_LEN + jnp.arange(T)
    q, k, v, qi, ki, wi = dsa_project(x, w_in, pos)
    n_pages = PAST_LEN // PAGE_SIZE
    past_ik = pool_ik[page_table].reshape(Bd, n_pages * PAGE_SIZE, IDX_DIM)
    all_ik = jnp.concatenate([past_ik, ki], axis=1)
    L = PAST_LEN + T
    topk = min(TOPK_MAX, L // 4)
    sc = indexer_scores(qi, wi, all_ik)
    sc = jnp.where(jnp.arange(L)[None, None, :] <= pos[None, :, None], sc, -jnp.inf)
    _, idx = lax.top_k(sc, topk)
    valid = idx <= pos[None, :, None]
    in_past = (idx < PAST_LEN)[..., None, None]
    pidx = jnp.minimum(idx, PAST_LEN - 1)
    phys_page = jax.vmap(lambda pt, ix: pt[ix])(page_table, pidx // PAGE_SIZE)
    flat = phys_page * PAGE_SIZE + pidx % PAGE_SIZE
    nidx = jnp.clip(idx - PAST_LEN, 0, T - 1)
    kg = jnp.where(in_past, pool_k.reshape(-1, KV_HEADS_B, HEAD_DIM)[flat], _gather_rows(k, nidx))
    vg = jnp.where(in_past, pool_v.reshape(-1, KV_HEADS_B, HEAD_DIM)[flat], _gather_rows(v, nidx))
    o = gathered_attention(q, kg, vg, valid)
    return o @ w_o, k, v, ki


def hier_moe(x, w_rg, b_rg, w_re, b_re, w_gu, w_dn):
    shp = x.shape
    xt = x.reshape(-1, D_MODEL)
    glog = (xt @ w_rg + b_rg).astype(jnp.float32)
    gprob = jax.nn.softmax(glog, axis=-1)
    g_sel = jnp.argmax(glog, axis=-1)
    elog = (xt @ w_re + b_re).astype(jnp.float32).reshape(-1, N_GROUPS, EXPERTS_PER_GROUP)
    elog_g = jnp.take_along_axis(elog, g_sel[:, None, None], axis=1)[:, 0]
    top_val, top_idx = lax.top_k(elog_g, TOP_K_IN_GROUP)
    gate = jax.nn.softmax(top_val, axis=-1) * jnp.take_along_axis(gprob, g_sel[:, None], axis=1)
    eid = g_sel[:, None] * EXPERTS_PER_GROUP + top_idx
    combine = jnp.einsum('nk,nke->ne', gate, jax.nn.one_hot(eid, N_EXPERTS, dtype=jnp.float32))
    gu = jnp.einsum('nd,edf->nef', xt, w_gu)
    g, u = jnp.split(gu, 2, axis=-1)
    h = jax.nn.silu(g) * u * combine.astype(x.dtype)[..., None]
    y = h.reshape(h.shape[0], N_EXPERTS * D_FF_EXPERT) @ w_dn.reshape(N_EXPERTS * D_FF_EXPERT, D_MODEL)
    return y.reshape(shp)


def setup_inputs(seed: int = 0) -> dict:
    key = jax.random.key(seed)
    ks = iter(jax.random.split(key, 40))

    def nrm(shape, scale):
        return jax.random.normal(next(ks), shape, jnp.float32) * scale

    n_pages = PAST_LEN // PAGE_SIZE
    n_used = DEC_BATCH * n_pages
    n_pool = n_used + max(1, n_used // 4)
    w_win = min(WINDOW, PAST_LEN)
    d_in = D_MODEL ** -0.5
    qd = N_HEADS * HEAD_DIM
    kva = KV_HEADS_A * HEAD_DIM
    kvb = KV_HEADS_B * HEAD_DIM

    x_prompt = nrm((BATCH, SEQ, D_MODEL), 1.0)
    x_sample = nrm((DEC_BATCH, DEC_SEQ, D_MODEL), 1.0)
    cache_win_k = nrm((N_A_LAYERS, DEC_BATCH, w_win, KV_HEADS_A, HEAD_DIM), 1.0)
    cache_win_v = nrm((N_A_LAYERS, DEC_BATCH, w_win, KV_HEADS_A, HEAD_DIM), 1.0)
    cache_k = nrm((N_B_LAYERS, n_pool, PAGE_SIZE, KV_HEADS_B, HEAD_DIM), 1.0)
    cache_v = nrm((N_B_LAYERS, n_pool, PAGE_SIZE, KV_HEADS_B, HEAD_DIM), 1.0)
    cache_idx_k = nrm((N_B_LAYERS, n_pool, PAGE_SIZE, IDX_DIM), 1.0)
    page_table = jax.random.permutation(next(ks), n_pool)[:n_used].reshape(DEC_BATCH, n_pages).astype(jnp.int32)

    a_w_qkv = jnp.concatenate([nrm((N_A_LAYERS, D_MODEL, qd), d_in),
                               nrm((N_A_LAYERS, D_MODEL, kva), d_in),
                               nrm((N_A_LAYERS, D_MODEL, kva), DEEPNORM_BETA * d_in)], axis=-1)
    a_b_qkv = nrm((N_A_LAYERS, qd + 2 * kva), 0.02)
    a_sinks = nrm((N_A_LAYERS, N_HEADS), 0.5)
    a_w_o = nrm((N_A_LAYERS, qd, D_MODEL), DEEPNORM_BETA * qd ** -0.5)

    b_w_in = jnp.concatenate([nrm((N_B_LAYERS, D_MODEL, qd), d_in),
                              nrm((N_B_LAYERS, D_MODEL, kvb), d_in),
                              nrm((N_B_LAYERS, D_MODEL, kvb), DEEPNORM_BETA * d_in),
                              nrm((N_B_LAYERS, D_MODEL, IDX_HEADS * IDX_DIM), d_in),
                              nrm((N_B_LAYERS, D_MODEL, IDX_DIM), d_in),
                              nrm((N_B_LAYERS, D_MODEL, IDX_HEADS), d_in)], axis=-1)
    b_w_o = nrm((N_B_LAYERS, qd, D_MODEL), DEEPNORM_BETA * qd ** -0.5)

    moe_w_rg = nrm((DEPTH, D_MODEL, N_GROUPS), d_in)
    moe_b_rg = nrm((DEPTH, N_GROUPS), 0.01)
    moe_w_re = nrm((DEPTH, D_MODEL, N_EXPERTS), d_in)
    moe_b_re = nrm((DEPTH, N_EXPERTS), 0.01)
    moe_w_gu = nrm((DEPTH, N_EXPERTS, D_MODEL, 2 * D_FF_EXPERT), DEEPNORM_BETA * d_in)
    moe_w_dn = nrm((DEPTH, N_EXPERTS, D_FF_EXPERT, D_MODEL), DEEPNORM_BETA * D_FF_EXPERT ** -0.5)

    ln1_g = 1.0 + nrm((DEPTH, D_MODEL), 0.02)
    ln1_b = nrm((DEPTH, D_MODEL), 0.02)
    ln2_g = 1.0 + nrm((DEPTH, D_MODEL), 0.02)
    ln2_b = nrm((DEPTH, D_MODEL), 0.02)

    return {"x_prompt": x_prompt, "x_sample": x_sample,
            "cache_win_k": cache_win_k, "cache_win_v": cache_win_v,
            "cache_k": cache_k, "cache_v": cache_v, "cache_idx_k": cache_idx_k,
            "page_table": page_table,
            "a_w_qkv": a_w_qkv, "a_b_qkv": a_b_qkv, "a_sinks": a_sinks, "a_w_o": a_w_o,
            "b_w_in": b_w_in, "b_w_o": b_w_o,
            "moe_w_rg": moe_w_rg, "moe_b_rg": moe_b_rg, "moe_w_re": moe_w_re, "moe_b_re": moe_b_re,
            "moe_w_gu": moe_w_gu, "moe_w_dn": moe_w_dn,
            "ln1_g": ln1_g, "ln1_b": ln1_b, "ln2_g": ln2_g, "ln2_b": ln2_b}


def reference(x_prompt, x_sample, cache_win_k, cache_win_v, cache_k, cache_v, cache_idx_k, page_table,
              a_w_qkv, a_b_qkv, a_sinks, a_w_o, b_w_in, b_w_o,
              moe_w_rg, moe_b_rg, moe_w_re, moe_b_re, moe_w_gu, moe_w_dn,
              ln1_g, ln1_b, ln2_g, ln2_b):
    yp, ys = x_prompt, x_sample
    wkp, wvp, wks, wvs = [], [], [], []
    kp, vp, ikp, kss, vss, iks = [], [], [], [], [], []
    for l in range(DEPTH):
        if l % 2 == 0:
            a = l // 2
            mp, k1, v1 = swa_prompt(yp, a_w_qkv[a], a_b_qkv[a], a_sinks[a], a_w_o[a])
            ms, k2, v2 = swa_sample(ys, cache_win_k[a], cache_win_v[a], a_w_qkv[a], a_b_qkv[a], a_sinks[a], a_w_o[a])
            wkp.append(k1); wvp.append(v1); wks.append(k2); wvs.append(v2)
        else:
            b = l // 2
            mp, k1, v1, i1 = dsa_prompt(yp, b_w_in[b], b_w_o[b])
            ms, k2, v2, i2 = dsa_sample(ys, cache_k[b], cache_v[b], cache_idx_k[b], page_table, b_w_in[b], b_w_o[b])
            kp.append(k1); vp.append(v1); ikp.append(i1); kss.append(k2); vss.append(v2); iks.append(i2)
        yp = layer_norm(DEEPNORM_ALPHA * yp + mp, ln1_g[l], ln1_b[l])
        ys = layer_norm(DEEPNORM_ALPHA * ys + ms, ln1_g[l], ln1_b[l])
        moe_args = (moe_w_rg[l], moe_b_rg[l], moe_w_re[l], moe_b_re[l], moe_w_gu[l], moe_w_dn[l])
        yp = layer_norm(DEEPNORM_ALPHA * yp + hier_moe(yp, *moe_args), ln2_g[l], ln2_b[l])
        ys = layer_norm(DEEPNORM_ALPHA * ys + hier_moe(ys, *moe_args), ln2_g[l], ln2_b[l])
    win_k_prompt = jnp.stack(wkp)
    win_v_prompt = jnp.stack(wvp)
    win_k_sample = jnp.stack(wks)
    win_v_sample = jnp.stack(wvs)
    k_prompt = jnp.stack(kp)
    v_prompt = jnp.stack(vp)
    idx_k_prompt = jnp.stack(ikp)
    k_sample = jnp.stack(kss)
    v_sample = jnp.stack(vss)
    idx_k_sample = jnp.stack(iks)
    return (yp, ys, win_k_prompt, win_v_prompt, win_k_sample, win_v_sample,
            k_prompt, v_prompt, idx_k_prompt, k_sample, v_sample, idx_k_sample)
```

```python
import functools
import math

import jax
import jax.numpy as jnp
import numpy as np
from jax import lax
from jax.experimental import pallas as pl
from jax.experimental.pallas import tpu as pltpu

D_MODEL = 1024
HEAD_DIM = 64
N_HEADS = 16
KV_HEADS_A = 2
KV_HEADS_B = 4
WINDOW = 128
BLOCK = 128
IDX_HEADS = 8
IDX_DIM = 64
TOPK_MAX = 256
N_GROUPS = 4
EXPERTS_PER_GROUP = 8
N_EXPERTS = N_GROUPS * EXPERTS_PER_GROUP
D_FF_EXPERT = 256
ROPE_THETA = 10000.0
LN_EPS = 1e-5
DEPTH = 2
DEEPNORM_ALPHA = (2 * DEPTH) ** 0.25
PAST_LEN = 8192
PAGE_SIZE = 128

LANES = 128
NEG_INF = float("-inf")
INT_MIN = -(2 ** 31)
VMEM_LIMIT = 56 * 1024 * 1024

F32 = jnp.float32
BF16 = jnp.bfloat16


def _slot_order(kv_heads):
    g = N_HEADS // kv_heads
    even = [h for h in range(N_HEADS) if (h // g) % 2 == 0]
    odd = [h for h in range(N_HEADS) if (h // g) % 2 == 1]
    order = []
    for a, b in zip(even, odd):
        order += [a, b]
    return order


def _col_perm(order):
    return np.concatenate([np.arange(h * HEAD_DIM, (h + 1) * HEAD_DIM) for h in order])


def _rope_tables(pos):
    half = HEAD_DIM // 2
    inv = jnp.exp(jnp.arange(half, dtype=F32) * (-2.0 * math.log(ROPE_THETA) / HEAD_DIM))
    ang = pos.astype(F32)[:, None] * inv[None, :]
    c, s = jnp.cos(ang), jnp.sin(ang)
    return jnp.concatenate([c, c, c, c], axis=1), jnp.concatenate([-s, s, -s, s], axis=1)


def _rope_slab(x, cos, sin, lane):
    swapped = jnp.where((lane % HEAD_DIM) < HEAD_DIM // 2,
                        pltpu.roll(x, LANES - HEAD_DIM // 2, axis=1),
                        pltpu.roll(x, HEAD_DIM // 2, axis=1))
    return x * cos + swapped * sin


def _nt(a, b):
    return lax.dot_general(a, b, (((1,), (1,)), ((), ())), preferred_element_type=F32)


def _nn(a, b):
    return jnp.dot(a, b, preferred_element_type=F32)


def _cparams(sem):
    return pltpu.CompilerParams(dimension_semantics=sem, vmem_limit_bytes=VMEM_LIMIT)


def _proj_kernel(x_ref, w_ref, b_ref, cos_ref, sin_ref, *out_refs, plan):
    xb = x_ref[...].astype(BF16)
    cos = cos_ref[...]
    sin = sin_ref[...]
    lane = lax.broadcasted_iota(jnp.int32, (x_ref.shape[0], LANES), 1)
    for (s0, ns, rope, scale, ois) in plan:
        acc = _nn(xb, w_ref[:, s0 * LANES:(s0 + ns) * LANES]) + b_ref[:, s0 * LANES:(s0 + ns) * LANES]
        for j in range(ns):
            slab = acc[:, j * LANES:(j + 1) * LANES]
            if rope:
                slab = _rope_slab(slab, cos, sin, lane)
            if scale != 1.0:
                slab = slab * scale
            for oi in ois:
                o_ref = out_refs[oi]
                o_ref[:, j * LANES:(j + 1) * LANES] = slab.astype(o_ref.dtype)


def _project(x, w, b, cos, sin, plan, outs, tile):
    n, d = x.shape
    f = w.shape[1]
    out_shape = [jax.ShapeDtypeStruct((n, wd), dt) for wd, dt in outs]
    out_specs = [pl.BlockSpec((tile, wd), lambda i: (i, 0)) for wd, _ in outs]
    return pl.pallas_call(
        functools.partial(_proj_kernel, plan=plan),
        grid=(n // tile,),
        in_specs=[pl.BlockSpec((tile, d), lambda i: (i, 0)),
                  pl.BlockSpec((d, f), lambda i: (0, 0)),
                  pl.BlockSpec((1, f), lambda i: (0, 0)),
                  pl.BlockSpec((tile, LANES), lambda i: (i, 0)),
                  pl.BlockSpec((tile, LANES), lambda i: (i, 0))],
        out_specs=out_specs,
        out_shape=out_shape,
        compiler_params=_cparams(("parallel",)),
        name="proj",
    )(x, w, b, cos, sin)


def _projT_kernel(x_ref, wt_ref, o_ref, *, scale):
    acc = _nt(wt_ref[...], x_ref[...].astype(BF16))
    if scale != 1.0:
        acc = acc * scale
    o_ref[...] = acc.astype(o_ref.dtype)


def _project_t(x, wt, scale, dtype, tile):
    n, d = x.shape
    f = wt.shape[0]
    return pl.pallas_call(
        functools.partial(_projT_kernel, scale=scale),
        grid=(n // tile,),
        in_specs=[pl.BlockSpec((tile, d), lambda i: (i, 0)),
                  pl.BlockSpec((f, d), lambda i: (0, 0))],
        out_specs=pl.BlockSpec((f, tile), lambda i: (0, i)),
        out_shape=jax.ShapeDtypeStruct((f, n), dtype),
        compiler_params=_cparams(("parallel",)),
        name="proj_t",
    )(x, wt)


def _swa_prompt_kernel(sink_ref, q_ref, kp_ref, kc_ref, vp_ref, vc_ref, o_ref):
    n = pl.program_id(1)
    kband = jnp.concatenate([kp_ref[...], kc_ref[...]], axis=0)
    vband = jnp.concatenate([vp_ref[...], vc_ref[...]], axis=0)
    qi = lax.broadcasted_iota(jnp.int32, (BLOCK, 2 * BLOCK), 0)
    kj = lax.broadcasted_iota(jnp.int32, (BLOCK, 2 * BLOCK), 1)
    rel = qi - kj + BLOCK
    valid = (rel >= 0) & (rel < WINDOW) & ((n > 0) | (kj >= BLOCK))
    lane = lax.broadcasted_iota(jnp.int32, (BLOCK, LANES), 1)
    low = lane < HEAD_DIM
    for pair in range(N_HEADS // 2):
        slab = q_ref[:, pair * LANES:(pair + 1) * LANES]
        outs = []
        for half in range(2):
            p = 2 * pair + half
            qe = jnp.where(low if half == 0 else ~low, slab, jnp.zeros_like(slab))
            s = jnp.where(valid, _nt(qe, kband), NEG_INF)
            sink = sink_ref[p]
            m = jnp.maximum(jnp.max(s, axis=1, keepdims=True), sink)
            pr = jnp.exp(s - m)
            den = jnp.sum(pr, axis=1, keepdims=True) + jnp.exp(sink - m)
            outs.append(_nn(pr.astype(BF16), vband) / den)
        o_ref[:, pair * LANES:(pair + 1) * LANES] = jnp.where(low, outs[0], outs[1]).astype(o_ref.dtype)


def _swa_prompt(q, kb, vb, sinks, batch, seq):
    nb = seq // BLOCK
    cur = lambda b, n, s: (b * nb + n, 0)
    prev = lambda b, n, s: (b * nb + jnp.maximum(n - 1, 0), 0)
    return pl.pallas_call(
        _swa_prompt_kernel,
        grid_spec=pltpu.PrefetchScalarGridSpec(
            num_scalar_prefetch=1, grid=(batch, nb),
            in_specs=[pl.BlockSpec((BLOCK, D_MODEL), cur),
                      pl.BlockSpec((BLOCK, LANES), prev), pl.BlockSpec((BLOCK, LANES), cur),
                      pl.BlockSpec((BLOCK, LANES), prev), pl.BlockSpec((BLOCK, LANES), cur)],
            out_specs=pl.BlockSpec((BLOCK, D_MODEL), cur)),
        out_shape=jax.ShapeDtypeStruct((batch * seq, D_MODEL), BF16),
        compiler_params=_cparams(("parallel", "parallel")),
        name="swa_prompt",
    )(sinks, q, kb, kb, vb, vb)


def _swa_sample_kernel(q_ref, kw_ref, kn_ref, vw_ref, vn_ref, sink_ref, o_ref, *, nb, dec_seq):
    w = kw_ref.shape[2]
    rows = q_ref.shape[1]
    lane = lax.broadcasted_iota(jnp.int32, (LANES, w), 0)
    lowf = lane < HEAD_DIM
    lane_n = lax.broadcasted_iota(jnp.int32, (LANES, LANES), 1)
    lown = lane_n < HEAD_DIM
    tok = lax.broadcasted_iota(jnp.int32, (rows, w), 0) % dec_seq
    col = lax.broadcasted_iota(jnp.int32, (rows, w), 1)
    valid_w = col > tok + (w - WINDOW)
    tok_n = lax.broadcasted_iota(jnp.int32, (rows, LANES), 0) % dec_seq
    col_n = lax.broadcasted_iota(jnp.int32, (rows, LANES), 1)
    valid_n = (col_n <= tok_n) & (col_n < dec_seq)
    lane_o = lax.broadcasted_iota(jnp.int32, (rows, LANES), 1)
    for i in range(nb):
        x = q_ref[i]
        kw = kw_ref[i].astype(BF16)
        vw = vw_ref[i].astype(BF16)
        kn = kn_ref[i].astype(BF16)
        vn = vn_ref[i].astype(BF16)
        outs = []
        for half in range(2):
            fm = lowf if half == 0 else ~lowf
            nm = lown if half == 0 else ~lown
            sw = jnp.where(valid_w, _nn(x, jnp.where(fm, kw, jnp.zeros_like(kw))), NEG_INF)
            sn = jnp.where(valid_n, _nt(x, jnp.where(nm, kn, jnp.zeros_like(kn))), NEG_INF)
            sink = sink_ref[:, half:half + 1]
            m = jnp.maximum(jnp.maximum(jnp.max(sw, axis=1, keepdims=True),
                                        jnp.max(sn, axis=1, keepdims=True)), sink)
            pw = jnp.exp(sw - m)
            pn = jnp.exp(sn - m)
            den = jnp.sum(pw, axis=1, keepdims=True) + jnp.sum(pn, axis=1, keepdims=True) + jnp.exp(sink - m)
            acc = _nt(pw.astype(BF16), vw) + _nn(pn.astype(BF16), vn)
            outs.append(acc / den)
        o_ref[i] = jnp.where(lane_o < HEAD_DIM, outs[0], outs[1]).astype(o_ref.dtype)


def _swa_sample(qx, kwt, knew, vwt, vnew, sinkmat, dec_seq, nb=8):
    bd, rows, _ = qx.shape
    w = kwt.shape[2]
    blk = lambda s: pl.BlockSpec((nb,) + s, lambda i: (i, 0, 0))
    return pl.pallas_call(
        functools.partial(_swa_sample_kernel, nb=nb, dec_seq=dec_seq),
        grid=(bd // nb,),
        in_specs=[blk((rows, LANES)), blk((LANES, w)), blk((LANES, LANES)), blk((LANES, w)), blk((LANES, LANES)),
                  pl.BlockSpec((rows, LANES), lambda i: (0, 0))],
        out_specs=blk((rows, LANES)),
        out_shape=jax.ShapeDtypeStruct((bd, rows, LANES), F32),
        compiler_params=_cparams(("parallel",)),
        name="swa_sample",
    )(qx, kwt, knew, vwt, vnew, sinkmat)


def _layer_norm(y, g, b):
    mu = jnp.mean(y, axis=1, keepdims=True)
    d = y - mu
    var = jnp.mean(d * d, axis=1, keepdims=True)
    return d * lax.rsqrt(var + LN_EPS) * g + b


def _out_ln_router_kernel(o_ref, x_ref, wo_ref, g_ref, b_ref, rh_ref, rl_ref, rb_ref, y_ref, comb_ref):
    y = _layer_norm(DEEPNORM_ALPHA * x_ref[...] + _nn(o_ref[...], wo_ref[...]), g_ref[...], b_ref[...])
    y_ref[...] = y
    yh = y.astype(BF16)
    yl = (y - yh.astype(F32)).astype(BF16)
    logits = _nn(yh, rh_ref[...]) + (_nn(yl, rh_ref[...]) + _nn(yh, rl_ref[...])) + rb_ref[...]
    lane = lax.broadcasted_iota(jnp.int32, logits.shape, 1)
    is_g = (lane >= N_EXPERTS) & (lane < N_EXPERTS + N_GROUPS)
    gl = jnp.where(is_g, logits, NEG_INF)
    gmax = jnp.max(gl, axis=1, keepdims=True)
    gsel = jnp.min(jnp.where(gl == gmax, lane, 2 * LANES), axis=1, keepdims=True) - N_EXPERTS
    gprob = 1.0 / jnp.sum(jnp.exp(gl - gmax), axis=1, keepdims=True)
    in_g = (lane >= gsel * EXPERTS_PER_GROUP) & (lane < (gsel + 1) * EXPERTS_PER_GROUP)
    el = jnp.where(in_g, logits, NEG_INF)
    v1 = jnp.max(el, axis=1, keepdims=True)
    i1 = jnp.min(jnp.where(el == v1, lane, 2 * LANES), axis=1, keepdims=True)
    el2 = jnp.where(lane == i1, NEG_INF, el)
    v2 = jnp.max(el2, axis=1, keepdims=True)
    i2 = jnp.min(jnp.where(el2 == v2, lane, 2 * LANES), axis=1, keepdims=True)
    e2 = jnp.exp(v2 - v1)
    den = 1.0 + e2
    comb_ref[...] = jnp.where(lane == i1, (1.0 / den) * gprob, jnp.where(lane == i2, (e2 / den) * gprob, 0.0))


def _out_ln_router(o, x, wo, g, b, rh, rl, rb, tile):
    n = x.shape[0]
    row = lambda i: (i, 0)
    fix = lambda i: (0, 0)
    return pl.pallas_call(
        _out_ln_router_kernel,
        grid=(n // tile,),
        in_specs=[pl.BlockSpec((tile, D_MODEL), row), pl.BlockSpec((tile, D_MODEL), row),
                  pl.BlockSpec((D_MODEL, D_MODEL), fix), pl.BlockSpec((1, D_MODEL), fix),
                  pl.BlockSpec((1, D_MODEL), fix), pl.BlockSpec((D_MODEL, LANES), fix),
                  pl.BlockSpec((D_MODEL, LANES), fix), pl.BlockSpec((1, LANES), fix)],
        out_specs=[pl.BlockSpec((tile, D_MODEL), row), pl.BlockSpec((tile, LANES), row)],
        out_shape=[jax.ShapeDtypeStruct((n, D_MODEL), F32), jax.ShapeDtypeStruct((n, LANES), F32)],
        compiler_params=_cparams(("parallel",)),
        name="out_ln_router",
    )(o, x, wo, g, b, rh, rl, rb)


def _moe_kernel(x_ref, comb_ref, wgu_ref, wdn_ref, g_ref, b_ref, y_ref, xb_ref, acc_ref, *, epb):
    j = pl.program_id(1)

    @pl.when(j == 0)
    def _():
        xb_ref[...] = x_ref[...].astype(BF16)
        acc_ref[...] = jnp.zeros_like(acc_ref)

    xb = xb_ref[...]
    comb = comb_ref[...]
    lane = lax.broadcasted_iota(jnp.int32, comb.shape, 1)
    for i in range(epb):
        e = j * epb + i
        gu = _nn(xb, wgu_ref[i])
        gate, up = gu[:, :D_FF_EXPERT], gu[:, D_FF_EXPERT:]
        w = jnp.sum(jnp.where(lane == e, comb, 0.0), axis=1, keepdims=True)
        h = (gate / (1.0 + jnp.exp(-gate))) * up * w
        acc_ref[...] += _nn(h.astype(BF16), wdn_ref[i])

    @pl.when(j == pl.num_programs(1) - 1)
    def _():
        y_ref[...] = _layer_norm(DEEPNORM_ALPHA * x_ref[...] + acc_ref[...], g_ref[...], b_ref[...])


def _moe(x, comb, wgu, wdn, g, b, tile, epb=4):
    n = x.shape[0]
    row = lambda i, j: (i, 0)
    fix = lambda i, j: (0, 0)
    return pl.pallas_call(
        functools.partial(_moe_kernel, epb=epb),
        grid=(n // tile, N_EXPERTS // epb),
        in_specs=[pl.BlockSpec((tile, D_MODEL), row), pl.BlockSpec((tile, LANES), row),
                  pl.BlockSpec((epb, D_MODEL, 2 * D_FF_EXPERT), lambda i, j: (j, 0, 0)),
                  pl.BlockSpec((epb, D_FF_EXPERT, D_MODEL), lambda i, j: (j, 0, 0)),
                  pl.BlockSpec((1, D_MODEL), fix), pl.BlockSpec((1, D_MODEL), fix)],
        out_specs=pl.BlockSpec((tile, D_MODEL), row),
        out_shape=jax.ShapeDtypeStruct((n, D_MODEL), F32),
        scratch_shapes=[pltpu.VMEM((tile, D_MODEL), BF16), pltpu.VMEM((tile, D_MODEL), F32)],
        compiler_params=_cparams(("parallel", "arbitrary")),
        name="moe",
    )(x, comb, wgu, wdn, g, b)


def _count(mask, axis):
    return jnp.sum(jnp.where(mask, 1.0, 0.0), axis=axis, keepdims=True)


def _order_key(x):
    bits = lax.bitcast_convert_type(x, jnp.int32)
    return jnp.where(bits < 0, bits ^ 0x7FFFFFFF, bits)


def _topk_mask(key, idx, k, axis, idx_bits):
    kf = float(k)
    r = jnp.where(_count(key >= 0, axis) >= kf, 0, INT_MIN).astype(jnp.int32)

    def value_bit(i, r):
        cand = r | jnp.left_shift(jnp.int32(1), 30 - i)
        return jnp.where(_count(key >= cand, axis) >= kf, cand, r)

    r = lax.fori_loop(0, 31, value_bit, r)
    above = key > r
    tie = key == r
    need = kf - _count(above, axis)

    def index_bit(i, j):
        cand = j | jnp.left_shift(jnp.int32(1), idx_bits - 1 - i)
        return jnp.where(_count(tie & (idx < cand), axis) < need, cand, j)

    j = lax.fori_loop(0, idx_bits, index_bit, jnp.zeros_like(r))
    return above | (tie & (idx <= j))


def _dsa_prompt_kernel(q_ref, kb_ref, vt_ref, kid_ref, qi_ref, wit_ref, o_ref, *, classes, topk, slot_kv):
    n = pl.program_id(1)
    for n0, cnt in classes:
        @pl.when((n >= n0) & (n < n0 + cnt))
        def _(n0=n0, cnt=cnt):
            _dsa_prompt_block(q_ref, kb_ref, vt_ref, kid_ref, qi_ref, wit_ref, o_ref,
                              sk=(n0 + cnt) * BLOCK, topk=topk, slot_kv=slot_kv)


def _dsa_prompt_block(q_ref, kb_ref, vt_ref, kid_ref, qi_ref, wit_ref, o_ref, *, sk, topk, slot_kv):
    t0 = pl.program_id(1) * BLOCK
    lane = lax.broadcasted_iota(jnp.int32, (BLOCK, LANES), 1)
    low = lane < HEAD_DIM
    kid = kid_ref[0:sk, :]
    sc = jnp.zeros((sk, BLOCK), F32)
    for h in range(IDX_HEADS):
        slab = qi_ref[:, (h // 2) * LANES:(h // 2 + 1) * LANES]
        qh = jnp.where(low if h % 2 == 0 else ~low, slab, jnp.zeros_like(slab))
        sc = sc + jnp.maximum(_nt(kid, qh), 0.0) * wit_ref[h:h + 1, :]
    s_idx = lax.broadcasted_iota(jnp.int32, (sk, BLOCK), 0)
    causal = s_idx <= t0 + lax.broadcasted_iota(jnp.int32, (sk, BLOCK), 1)
    key = _order_key(jnp.where(causal, sc, NEG_INF))
    sel = _topk_mask(key, s_idx, topk, 0, (sk - 1).bit_length()) & causal
    kb = kb_ref[0:sk, :]
    outs = []
    for p in range(N_HEADS):
        kv = slot_kv[p]
        slab = q_ref[:, (p // 2) * LANES:(p // 2 + 1) * LANES]
        qm = jnp.where(low if p % 2 == 0 else ~low, slab, jnp.zeros_like(slab))
        z = jnp.zeros_like(qm)
        qe = jnp.concatenate([qm, z] if kv // 2 == 0 else [z, qm], axis=1)
        s = jnp.where(sel, _nt(kb, qe), NEG_INF)
        m = jnp.max(s, axis=0, keepdims=True)
        pr = jnp.exp(s - m)
        den = jnp.sum(pr, axis=0, keepdims=True)
        ot = _nn(vt_ref[kv * HEAD_DIM:(kv + 1) * HEAD_DIM, 0:sk], pr.astype(BF16))
        outs.append(ot / den)
    o_ref[...] = jnp.concatenate(outs, axis=0).T.astype(o_ref.dtype)


def _dsa_prompt(q, kb, vt, kid, qi, wit, batch, seq, topk, slot_kv, n_classes=4):
    nb = seq // BLOCK
    per = -(-nb // n_classes)
    classes = tuple((n0, min(per, nb - n0)) for n0 in range(0, nb, per))
    qmap = lambda b, n: (b * nb + n, 0)
    return pl.pallas_call(
        functools.partial(_dsa_prompt_kernel, classes=classes, topk=topk, slot_kv=slot_kv),
        grid=(batch, nb),
        in_specs=[pl.BlockSpec((BLOCK, D_MODEL), qmap),
                  pl.BlockSpec((seq, 2 * LANES), lambda b, n: (b, 0)),
                  pl.BlockSpec((2 * LANES, seq), lambda b, n: (0, b)),
                  pl.BlockSpec((seq, LANES), lambda b, n: (b, 0)),
                  pl.BlockSpec((BLOCK, IDX_HEADS * IDX_DIM), qmap),
                  pl.BlockSpec((IDX_HEADS, BLOCK), lambda b, n: (0, b * nb + n))],
        out_specs=pl.BlockSpec((BLOCK, D_MODEL), qmap),
        out_shape=jax.ShapeDtypeStruct((batch * seq, D_MODEL), BF16),
        compiler_params=_cparams(("parallel", "arbitrary")),
        name="dsa_prompt",
    )(q, kb, vt, kid, qi, wit)


def _page_copies(pt_ref, pool_ref, buf_ref, sem, step, slot, pairb, page0, n_pages):
    cps = []
    for bi in range(pairb):
        for i in range(n_pages):
            page = pt_ref[step * pairb + bi, page0 + i]
            cps.append(pltpu.make_async_copy(
                pool_ref.at[page], buf_ref.at[slot, bi, :, pl.ds(i * PAGE_SIZE, PAGE_SIZE)], sem.at[slot]))
    return cps


def _dsa_sample_scores_kernel(pt_ref, pool_ref, qi_ref, w_ref, kin_ref, o_ref, buf_ref, sem, *, pairb, n_pages, dec_seq):
    g = pl.program_id(0)
    ng = pl.num_programs(0)
    slot = g % 2

    @pl.when(g == 0)
    def _():
        for cp in _page_copies(pt_ref, pool_ref, buf_ref, sem, 0, 0, pairb, 0, n_pages):
            cp.start()

    @pl.when(g + 1 < ng)
    def _():
        for cp in _page_copies(pt_ref, pool_ref, buf_ref, sem, g + 1, 1 - slot, pairb, 0, n_pages):
            cp.start()

    for cp in _page_copies(pt_ref, pool_ref, buf_ref, sem, g, slot, pairb, 0, n_pages):
        cp.wait()

    past = n_pages * PAGE_SIZE
    lane = lax.broadcasted_iota(jnp.int32, (1, LANES), 1)
    for bi in range(pairb):
        qi = qi_ref[bi]
        w = w_ref[bi]
        r = jnp.maximum(_nn(qi, buf_ref[slot, bi].astype(BF16)), 0.0) * w
        rn = jnp.maximum(_nt(qi, kin_ref[bi]), 0.0) * w
        for t in range(dec_seq):
            row = bi * dec_seq + t
            o_ref[row:row + 1, 0:past] = jnp.sum(r[t * IDX_HEADS:(t + 1) * IDX_HEADS], axis=0, keepdims=True)
            new = jnp.sum(rn[t * IDX_HEADS:(t + 1) * IDX_HEADS], axis=0, keepdims=True)
            o_ref[row:row + 1, past:past + LANES] = jnp.where(lane <= t, new, NEG_INF)


def _dsa_sample_scores(page_table, pool_t, qix, wcol, kin, dec_seq, pairb=2):
    bd, n_pages = page_table.shape
    past = n_pages * PAGE_SIZE
    rows = dec_seq * IDX_HEADS
    blk = lambda s: pl.BlockSpec((pairb,) + s, lambda g, pt: (g, 0, 0))
    return pl.pallas_call(
        functools.partial(_dsa_sample_scores_kernel, pairb=pairb, n_pages=n_pages, dec_seq=dec_seq),
        grid_spec=pltpu.PrefetchScalarGridSpec(
            num_scalar_prefetch=1, grid=(bd // pairb,),
            in_specs=[pl.BlockSpec(memory_space=pl.ANY), blk((rows, IDX_DIM)), blk((rows, 1)), blk((LANES, IDX_DIM))],
            out_specs=pl.BlockSpec((pairb * dec_seq, past + LANES), lambda g, pt: (g, 0)),
            scratch_shapes=[pltpu.VMEM((2, pairb, IDX_DIM, past), F32), pltpu.SemaphoreType.DMA((2,))]),
        out_shape=jax.ShapeDtypeStruct((bd * dec_seq, past + LANES), F32),
        compiler_params=_cparams(("arbitrary",)),
        name="dsa_sample_scores",
    )(page_table, pool_t, qix, wcol, kin)


def _select_bias_kernel(sc_ref, o_ref, *, topk):
    sc = sc_ref[...]
    idx = lax.broadcasted_iota(jnp.int32, sc.shape, 1)
    sel = _topk_mask(_order_key(sc), idx, topk, 1, (sc.shape[1] - 1).bit_length())
    o_ref[...] = jnp.where(sel & (sc > NEG_INF), 0.0, NEG_INF)


def _select_bias(sc, topk, tile):
    r, c = sc.shape
    return pl.pallas_call(
        functools.partial(_select_bias_kernel, topk=topk),
        grid=(r // tile,),
        in_specs=[pl.BlockSpec((tile, c), lambda i: (i, 0))],
        out_specs=pl.BlockSpec((tile, c), lambda i: (i, 0)),
        out_shape=jax.ShapeDtypeStruct((r, c), F32),
        compiler_params=_cparams(("parallel",)),
        name="select_bias",
    )(sc)


def _dsa_sample_attn_kernel(pt_ref, kpool_ref, vpool_ref, qe_ref, bias_ref, biasn_ref, kn_ref, vn_ref, o_ref,
                            kbuf, vbuf, ksem, vsem, m_ref, l_ref, acc_ref, *, pairb, chunk_pages, dec_seq):
    g = pl.program_id(0)
    c = pl.program_id(1)
    nch = pl.num_programs(1)
    lin = g * nch + c
    total = pl.num_programs(0) * nch
    slot = lin % 2

    def copies(step_lin, slot_):
        gg = step_lin // nch
        cc = step_lin % nch
        return (_page_copies(pt_ref, kpool_ref, kbuf, ksem, gg, slot_, pairb, cc * chunk_pages, chunk_pages)
                + _page_copies(pt_ref, vpool_ref, vbuf, vsem, gg, slot_, pairb, cc * chunk_pages, chunk_pages))

    @pl.when(lin == 0)
    def _():
        for cp in copies(0, 0):
            cp.start()

    @pl.when(lin + 1 < total)
    def _():
        for cp in copies(lin + 1, 1 - slot):
            cp.start()

    for cp in copies(lin, slot):
        cp.wait()

    @pl.when(c == 0)
    def _():
        m_ref[...] = jnp.full_like(m_ref, -1e30)
        l_ref[...] = jnp.zeros_like(l_ref)
        acc_ref[...] = jnp.zeros_like(acc_ref)

    rows = qe_ref.shape[1]
    per_tok = rows // dec_seq

    def expand(b4):
        return jnp.concatenate(
            [jnp.broadcast_to(b4[t:t + 1, :], (per_tok, b4.shape[1])) for t in range(dec_seq)], axis=0)

    def update(bi, s, pv):
        m_old = m_ref[bi]
        m_new = jnp.maximum(m_old, jnp.max(s, axis=1, keepdims=True))
        alpha = jnp.exp(m_old - m_new)
        pr = jnp.exp(s - m_new)
        l_ref[bi] = alpha * l_ref[bi] + jnp.sum(pr, axis=1, keepdims=True)
        acc_ref[bi] = alpha * acc_ref[bi] + pv(pr.astype(BF16))
        m_ref[bi] = m_new

    for bi in range(pairb):
        qe = qe_ref[bi]
        kt = kbuf[slot, bi].astype(BF16)
        vt = vbuf[slot, bi].astype(BF16)
        s = _nn(qe, kt) + expand(bias_ref[bi * dec_seq:(bi + 1) * dec_seq, :])
        update(bi, s, lambda pr: _nt(pr, vt))

    @pl.when(c == nch - 1)
    def _():
        for bi in range(pairb):
            qe = qe_ref[bi]
            s = _nt(qe, kn_ref[bi]) + expand(biasn_ref[bi * dec_seq:(bi + 1) * dec_seq, :])
            update(bi, s, lambda pr: _nn(pr, vn_ref[bi]))
            o_ref[bi] = acc_ref[bi] / l_ref[bi]


def _dsa_sample_attn(page_table, kpool_t, vpool_t, qe, bias, knew, vnew, dec_seq, pairb=2, chunk_pages=16):
    bd, n_pages = page_table.shape
    nch = n_pages // chunk_pages
    chunk = chunk_pages * PAGE_SIZE
    rows = qe.shape[1]
    kvw = qe.shape[2]
    blk = lambda s: pl.BlockSpec((pairb,) + s, lambda g, c, pt: (g, 0, 0))
    return pl.pallas_call(
        functools.partial(_dsa_sample_attn_kernel, pairb=pairb, chunk_pages=chunk_pages, dec_seq=dec_seq),
        grid_spec=pltpu.PrefetchScalarGridSpec(
            num_scalar_prefetch=1, grid=(bd // pairb, nch),
            in_specs=[pl.BlockSpec(memory_space=pl.ANY), pl.BlockSpec(memory_space=pl.ANY),
                      blk((rows, kvw)),
                      pl.BlockSpec((pairb * dec_seq, chunk), lambda g, c, pt: (g, c)),
                      pl.BlockSpec((pairb * dec_seq, LANES), lambda g, c, pt: (g, n_pages)),
                      blk((LANES, kvw)), blk((LANES, kvw))],
            out_specs=blk((rows, kvw)),
            scratch_shapes=[pltpu.VMEM((2, pairb, kvw, chunk), F32), pltpu.VMEM((2, pairb, kvw, chunk), F32),
                            pltpu.SemaphoreType.DMA((2,)), pltpu.SemaphoreType.DMA((2,)),
                            pltpu.VMEM((pairb, rows, 1), F32), pltpu.VMEM((pairb, rows, 1), F32),
                            pltpu.VMEM((pairb, rows, kvw), F32)]),
        out_shape=jax.ShapeDtypeStruct((bd, rows, kvw), F32),
        compiler_params=_cparams(("arbitrary", "arbitrary")),
        name="dsa_sample_attn",
    )(page_table, kpool_t, vpool_t, qe, bias, bias, knew, vnew)


def _pick_tile(n, candidates):
    for t in candidates:
        if n % t == 0:
            return t
    raise ValueError(f"no token tile for {n}")


def _hi_lo(w):
    hi = w.astype(BF16)
    return hi, (w - hi.astype(F32)).astype(BF16)


def _pad_rows(x, rows):
    return jnp.pad(x, ((0, 0), (0, rows - x.shape[1]), (0, 0)))


def _post_attention(o, x, w_o_phys, l, moe, ln, tile, moe_tile):
    (w_rg, b_rg, w_re, b_re, w_gu, w_dn) = moe
    (ln1_g, ln1_b, ln2_g, ln2_b) = ln
    rw = jnp.zeros((D_MODEL, LANES), F32).at[:, :N_EXPERTS].set(w_re[l])
    rw = rw.at[:, N_EXPERTS:N_EXPERTS + N_GROUPS].set(w_rg[l])
    rb = jnp.zeros((1, LANES), F32).at[0, :N_EXPERTS].set(b_re[l]).at[0, N_EXPERTS:N_EXPERTS + N_GROUPS].set(b_rg[l])
    rh, rl = _hi_lo(rw)
    y, comb = _out_ln_router(o, x, w_o_phys.astype(BF16), ln1_g[l].reshape(1, -1), ln1_b[l].reshape(1, -1),
                             rh, rl, rb, tile)
    return _moe(y, comb, w_gu[l].astype(BF16), w_dn[l].astype(BF16),
                ln2_g[l].reshape(1, -1), ln2_b[l].reshape(1, -1), moe_tile)


def kernel(x_prompt, x_sample, cache_win_k, cache_win_v, cache_k, cache_v, cache_idx_k, page_table,
           a_w_qkv, a_b_qkv, a_sinks, a_w_o, b_w_in, b_w_o,
           moe_w_rg, moe_b_rg, moe_w_re, moe_b_re, moe_w_gu, moe_w_dn,
           ln1_g, ln1_b, ln2_g, ln2_b):
    B, S, _ = x_prompt.shape
    Bd, T, _ = x_sample.shape
    n_p, n_s = B * S, Bd * T
    n_tot = n_p + n_s
    past = page_table.shape[1] * PAGE_SIZE
    tile = _pick_tile(n_tot, (512, 256, 128, 64, 32, 16, 8))
    moe_tile = tile
    qd = N_HEADS * HEAD_DIM
    moe = (moe_w_rg, moe_b_rg, moe_w_re, moe_b_re, moe_w_gu, moe_w_dn)
    ln = (ln1_g, ln1_b, ln2_g, ln2_b)

    x = jnp.concatenate([x_prompt.reshape(n_p, D_MODEL), x_sample.reshape(n_s, D_MODEL)], axis=0)
    pos = jnp.concatenate([jnp.tile(jnp.arange(S, dtype=jnp.int32), B),
                           jnp.tile(past + jnp.arange(T, dtype=jnp.int32), Bd)])
    cos, sin = _rope_tables(pos)
    scale = HEAD_DIM ** -0.5

    a = 0
    kva = KV_HEADS_A * HEAD_DIM
    order_a = _slot_order(KV_HEADS_A)
    perm_a = _col_perm(order_a)
    wq, wk, wv = a_w_qkv[a][:, :qd], a_w_qkv[a][:, qd:qd + kva], a_w_qkv[a][:, qd + kva:]
    bq, bk, bv = a_b_qkv[a][:qd], a_b_qkv[a][qd:qd + kva], a_b_qkv[a][qd + kva:]
    w_a = jnp.concatenate([wq[:, perm_a], wk, wv], axis=1).astype(BF16)
    b_a = jnp.concatenate([bq[perm_a], bk, bv]).reshape(1, -1)
    plan_a = ((0, 8, True, scale, (0,)), (8, 1, True, 1.0, (1, 2)), (9, 1, False, 1.0, (3, 4)))
    q0, k0, k0b, v0, v0b = _project(x, w_a, b_a, cos, sin, plan_a,
                                    [(qd, BF16), (kva, F32), (kva, BF16), (kva, F32), (kva, BF16)], tile)
    sinks_phys = a_sinks[a][np.asarray(order_a)]
    o_p = _swa_prompt(q0, k0b, v0b, sinks_phys, B, S)

    w_win = cache_win_k.shape[2]
    qx = q0[n_p:].reshape(Bd, T, N_HEADS // 2, LANES).transpose(0, 2, 1, 3).reshape(Bd, T * N_HEADS // 2, LANES)
    kwt = cache_win_k[a].transpose(0, 2, 3, 1).reshape(Bd, kva, w_win)
    vwt = cache_win_v[a].transpose(0, 2, 3, 1).reshape(Bd, kva, w_win)
    k0s = k0[n_p:].reshape(Bd, T, kva)
    v0s = v0[n_p:].reshape(Bd, T, kva)
    sinkmat = jnp.zeros((T * N_HEADS // 2, LANES), F32).at[:, :2].set(
        jnp.repeat(sinks_phys.reshape(N_HEADS // 2, 2), T, axis=0))
    ox = _swa_sample(qx, kwt, _pad_rows(k0s, LANES), vwt, _pad_rows(v0s, LANES), sinkmat, T,
                     nb=_pick_tile(Bd, (8, 4, 2, 1)))
    o_s = ox.reshape(Bd, N_HEADS // 2, T, LANES).transpose(0, 2, 1, 3).reshape(n_s, qd).astype(BF16)

    x = _post_attention(jnp.concatenate([o_p, o_s], axis=0), x, a_w_o[a][perm_a, :], 0, moe, ln, tile, moe_tile)

    wp = min(WINDOW, S)
    win_k_prompt = k0[:n_p].reshape(B, S, KV_HEADS_A, HEAD_DIM)[:, S - wp:][None]
    win_v_prompt = v0[:n_p].reshape(B, S, KV_HEADS_A, HEAD_DIM)[:, S - wp:][None]
    win_k_sample = jnp.concatenate([cache_win_k[a], k0s.reshape(Bd, T, KV_HEADS_A, HEAD_DIM)], axis=1)[:, -w_win:][None]
    win_v_sample = jnp.concatenate([cache_win_v[a], v0s.reshape(Bd, T, KV_HEADS_A, HEAD_DIM)], axis=1)[:, -w_win:][None]

    bl = 0
    kvb = KV_HEADS_B * HEAD_DIM
    qid = IDX_HEADS * IDX_DIM
    order_b = _slot_order(KV_HEADS_B)
    perm_b = _col_perm(order_b)
    slot_kv = tuple(h // (N_HEADS // KV_HEADS_B) for h in order_b)
    w_in = b_w_in[bl]
    c0 = 0
    wq = w_in[:, c0:c0 + qd]; c0 += qd
    wk = w_in[:, c0:c0 + kvb]; c0 += kvb
    wv = w_in[:, c0:c0 + kvb]; c0 += kvb
    wqi = w_in[:, c0:c0 + qid]; c0 += qid
    wki = w_in[:, c0:c0 + IDX_DIM]; c0 += IDX_DIM
    wwi = w_in[:, c0:c0 + IDX_HEADS]
    w_b = jnp.concatenate([wq[:, perm_b], wk, wv, wqi, wki, wki], axis=1).astype(BF16)
    b_b = jnp.zeros((1, w_b.shape[1]), F32)
    plan_b = ((0, 8, True, scale, (0,)), (8, 2, True, 1.0, (1, 2)), (10, 2, False, 1.0, (3,)),
              (12, 4, True, IDX_DIM ** -0.5, (4,)), (16, 1, True, 1.0, (5, 6)))
    q1, k1, k1b, v1, qi1, ki1, kid1 = _project(
        x, w_b, b_b, cos, sin, plan_b,
        [(qd, BF16), (kvb, F32), (kvb, BF16), (kvb, F32), (qid, BF16), (LANES, F32), (LANES, BF16)], tile)
    vt1 = _project_t(x, wv.T.astype(BF16), 1.0, BF16, tile)
    wwi_t = jnp.zeros((2 * IDX_HEADS, D_MODEL), F32).at[:IDX_HEADS].set(wwi.T).astype(BF16)
    wit1 = _project_t(x, wwi_t, IDX_HEADS ** -0.5, F32, tile)

    o_p = _dsa_prompt(q1, k1b, vt1, kid1, qi1, wit1, B, S, min(TOPK_MAX, S // 4), slot_kv)

    n_pool = cache_k.shape[1]
    pool_ik_t = cache_idx_k[bl].transpose(0, 2, 1)
    qix = qi1[n_p:].reshape(Bd, T * IDX_HEADS, IDX_DIM)
    wcol = wit1[:IDX_HEADS, n_p:].T.reshape(Bd, T * IDX_HEADS, 1)
    kin = _pad_rows(kid1[n_p:, :IDX_DIM].reshape(Bd, T, IDX_DIM), LANES)
    sc = _dsa_sample_scores(page_table, pool_ik_t, qix, wcol, kin, T)
    bias = _select_bias(sc, min(TOPK_MAX, (past + T) // 4), _pick_tile(n_s, (128, 64, 32, 16, 8)))
    kpool_t = cache_k[bl].transpose(0, 2, 3, 1).reshape(n_pool, kvb, PAGE_SIZE)
    vpool_t = cache_v[bl].transpose(0, 2, 3, 1).reshape(n_pool, kvb, PAGE_SIZE)
    onehot = (np.asarray(slot_kv)[:, None] == np.arange(KV_HEADS_B)[None, :]).astype(np.float32)
    qe = (q1[n_p:].reshape(n_s, N_HEADS, 1, HEAD_DIM) * jnp.asarray(onehot, BF16)[None, :, :, None])
    qe = qe.reshape(Bd, T * N_HEADS, kvb)
    knew = _pad_rows(k1b[n_p:].reshape(Bd, T, kvb), LANES)
    vnew = _pad_rows(v1[n_p:].astype(BF16).reshape(Bd, T, kvb), LANES)
    ox = _dsa_sample_attn(page_table, kpool_t, vpool_t, qe, bias, knew, vnew, T,
                          chunk_pages=_pick_tile(page_table.shape[1], (16, 8, 4, 2, 1)))
    ox = ox.reshape(Bd, T, N_HEADS, KV_HEADS_B, HEAD_DIM)
    o_s = ox[:, :, np.arange(N_HEADS), np.asarray(slot_kv), :].reshape(n_s, qd).astype(BF16)

    x = _post_attention(jnp.concatenate([o_p, o_s], axis=0), x, b_w_o[bl][perm_b, :], 1, moe, ln, tile, moe_tile)

    k_prompt = k1[:n_p].reshape(B, S, KV_HEADS_B, HEAD_DIM)[None]
    v_prompt = v1[:n_p].reshape(B, S, KV_HEADS_B, HEAD_DIM)[None]
    idx_k_prompt = ki1[:n_p, :IDX_DIM].reshape(B, S, IDX_DIM)[None]
    k_sample = k1[n_p:].reshape(Bd, T, KV_HEADS_B, HEAD_DIM)[None]
    v_sample = v1[n_p:].reshape(Bd, T, KV_HEADS_B, HEAD_DIM)[None]
    idx_k_sample = ki1[n_p:, :IDX_DIM].reshape(Bd, T, IDX_DIM)[None]

    return (x[:n_p].reshape(B, S, D_MODEL), x[n_p:].reshape(Bd, T, D_MODEL),
            win_k_prompt, win_v_prompt, win_k_sample, win_v_sample,
            k_prompt, v_prompt, idx_k_prompt, k_sample, v_sample, idx_k_sample)
```

```python
import functools
import math

import jax
import jax.numpy as jnp
import numpy as np
from jax import lax
from jax.experimental import pallas as pl
from jax.experimental.pallas import tpu as pltpu

D_MODEL = 1024
HEAD_DIM = 64
N_HEADS = 16
KV_HEADS_A = 2
KV_HEADS_B = 4
WINDOW = 128
BLOCK = 128
IDX_HEADS = 8
IDX_DIM = 64
TOPK_MAX = 256
N_GROUPS = 4
EXPERTS_PER_GROUP = 8
N_EXPERTS = N_GROUPS * EXPERTS_PER_GROUP
D_FF_EXPERT = 256
ROPE_THETA = 10000.0
LN_EPS = 1e-5
DEPTH = 2
DEEPNORM_ALPHA = (2 * DEPTH) ** 0.25
PAST_LEN = 8192
PAGE_SIZE = 128

LANES = 128
NEG_INF = float("-inf")
INT_MIN = -(2 ** 31)
VMEM_LIMIT = 56 * 1024 * 1024

F32 = jnp.float32
BF16 = jnp.bfloat16


def _slot_order(kv_heads):
    g = N_HEADS // kv_heads
    even = [h for h in range(N_HEADS) if (h // g) % 2 == 0]
    odd = [h for h in range(N_HEADS) if (h // g) % 2 == 1]
    order = []
    for a, b in zip(even, odd):
        order += [a, b]
    return order


def _col_perm(order):
    return np.concatenate([np.arange(h * HEAD_DIM, (h + 1) * HEAD_DIM) for h in order])


def _rope_tables(pos):
    half = HEAD_DIM // 2
    inv = jnp.exp(jnp.arange(half, dtype=F32) * (-2.0 * math.log(ROPE_THETA) / HEAD_DIM))
    ang = pos.astype(F32)[:, None] * inv[None, :]
    c, s = jnp.cos(ang), jnp.sin(ang)
    return jnp.concatenate([c, c, c, c], axis=1), jnp.concatenate([-s, s, -s, s], axis=1)


def _rope_slab(x, cos, sin, lane):
    swapped = jnp.where((lane % HEAD_DIM) < HEAD_DIM // 2,
                        pltpu.roll(x, LANES - HEAD_DIM // 2, axis=1),
                        pltpu.roll(x, HEAD_DIM // 2, axis=1))
    return x * cos + swapped * sin


def _nt(a, b):
    return lax.dot_general(a, b, (((1,), (1,)), ((), ())), preferred_element_type=F32)


def _nn(a, b):
    return jnp.dot(a, b, preferred_element_type=F32)


def _cparams(sem):
    return pltpu.CompilerParams(dimension_semantics=sem, vmem_limit_bytes=VMEM_LIMIT)


def _proj_kernel(x_ref, w_ref, b_ref, cos_ref, sin_ref, *out_refs, plan):
    xb = x_ref[...].astype(BF16)
    cos = cos_ref[...]
    sin = sin_ref[...]
    lane = lax.broadcasted_iota(jnp.int32, (x_ref.shape[0], LANES), 1)
    for (s0, ns, rope, scale, ois) in plan:
        acc = _nn(xb, w_ref[:, s0 * LANES:(s0 + ns) * LANES]) + b_ref[:, s0 * LANES:(s0 + ns) * LANES]
        for j in range(ns):
            slab = acc[:, j * LANES:(j + 1) * LANES]
            if rope:
                slab = _rope_slab(slab, cos, sin, lane)
            if scale != 1.0:
                slab = slab * scale
            for oi in ois:
                o_ref = out_refs[oi]
                o_ref[:, j * LANES:(j + 1) * LANES] = slab.astype(o_ref.dtype)


def _project(x, w, b, cos, sin, plan, outs, tile):
    n, d = x.shape
    f = w.shape[1]
    out_shape = [jax.ShapeDtypeStruct((n, wd), dt) for wd, dt in outs]
    out_specs = [pl.BlockSpec((tile, wd), lambda i: (i, 0)) for wd, _ in outs]
    return pl.pallas_call(
        functools.partial(_proj_kernel, plan=plan),
        grid=(n // tile,),
        in_specs=[pl.BlockSpec((tile, d), lambda i: (i, 0)),
                  pl.BlockSpec((d, f), lambda i: (0, 0)),
                  pl.BlockSpec((1, f), lambda i: (0, 0)),
                  pl.BlockSpec((tile, LANES), lambda i: (i, 0)),
                  pl.BlockSpec((tile, LANES), lambda i: (i, 0))],
        out_specs=out_specs,
        out_shape=out_shape,
        compiler_params=_cparams(("parallel",)),
        name="proj",
    )(x, w, b, cos, sin)


def _projT_kernel(x_ref, wt_ref, o_ref, *, scale):
    acc = _nt(wt_ref[...], x_ref[...].astype(BF16))
    if scale != 1.0:
        acc = acc * scale
    o_ref[...] = acc.astype(o_ref.dtype)


def _project_t(x, wt, scale, dtype, tile):
    n, d = x.shape
    f = wt.shape[0]
    return pl.pallas_call(
        functools.partial(_projT_kernel, scale=scale),
        grid=(n // tile,),
        in_specs=[pl.BlockSpec((tile, d), lambda i: (i, 0)),
                  pl.BlockSpec((f, d), lambda i: (0, 0))],
        out_specs=pl.BlockSpec((f, tile), lambda i: (0, i)),
        out_shape=jax.ShapeDtypeStruct((f, n), dtype),
        compiler_params=_cparams(("parallel",)),
        name="proj_t",
    )(x, wt)


def _swa_prompt_kernel(sink_ref, q_ref, kp_ref, kc_ref, vp_ref, vc_ref, o_ref):
    n = pl.program_id(1)
    kband = jnp.concatenate([kp_ref[...], kc_ref[...]], axis=0)
    vband = jnp.concatenate([vp_ref[...], vc_ref[...]], axis=0)
    qi = lax.broadcasted_iota(jnp.int32, (BLOCK, 2 * BLOCK), 0)
    kj = lax.broadcasted_iota(jnp.int32, (BLOCK, 2 * BLOCK), 1)
    rel = qi - kj + BLOCK
    valid = (rel >= 0) & (rel < WINDOW) & ((n > 0) | (kj >= BLOCK))
    lane = lax.broadcasted_iota(jnp.int32, (BLOCK, LANES), 1)
    low = lane < HEAD_DIM
    for pair in range(N_HEADS // 2):
        slab = q_ref[:, pair * LANES:(pair + 1) * LANES]
        outs = []
        for half in range(2):
            p = 2 * pair + half
            qe = jnp.where(low if half == 0 else ~low, slab, jnp.zeros_like(slab))
            s = jnp.where(valid, _nt(qe, kband), NEG_INF)
            sink = sink_ref[p]
            m = jnp.maximum(jnp.max(s, axis=1, keepdims=True), sink)
            pr = jnp.exp(s - m)
            den = jnp.sum(pr, axis=1, keepdims=True) + jnp.exp(sink - m)
            outs.append(_nn(pr.astype(BF16), vband) / den)
        o_ref[:, pair * LANES:(pair + 1) * LANES] = jnp.where(low, outs[0], outs[1]).astype(o_ref.dtype)


def _swa_prompt(q, kb, vb, sinks, batch, seq):
    nb = seq // BLOCK
    cur = lambda b, n, s: (b * nb + n, 0)
    prev = lambda b, n, s: (b * nb + jnp.maximum(n - 1, 0), 0)
    return pl.pallas_call(
        _swa_prompt_kernel,
        grid_spec=pltpu.PrefetchScalarGridSpec(
            num_scalar_prefetch=1, grid=(batch, nb),
            in_specs=[pl.BlockSpec((BLOCK, D_MODEL), cur),
                      pl.BlockSpec((BLOCK, LANES), prev), pl.BlockSpec((BLOCK, LANES), cur),
                      pl.BlockSpec((BLOCK, LANES), prev), pl.BlockSpec((BLOCK, LANES), cur)],
            out_specs=pl.BlockSpec((BLOCK, D_MODEL), cur)),
        out_shape=jax.ShapeDtypeStruct((batch * seq, D_MODEL), BF16),
        compiler_params=_cparams(("parallel", "parallel")),
        name="swa_prompt",
    )(sinks, q, kb, kb, vb, vb)


def _swa_sample_kernel(q_ref, kw_ref, kn_ref, vw_ref, vn_ref, sink_ref, o_ref, *, nb, dec_seq):
    w = kw_ref.shape[2]
    rows = q_ref.shape[1]
    lane = lax.broadcasted_iota(jnp.int32, (LANES, w), 0)
    lowf = lane < HEAD_DIM
    lane_n = lax.broadcasted_iota(jnp.int32, (LANES, LANES), 1)
    lown = lane_n < HEAD_DIM
    tok = lax.broadcasted_iota(jnp.int32, (rows, w), 0) % dec_seq
    col = lax.broadcasted_iota(jnp.int32, (rows, w), 1)
    valid_w = col > tok + (w - WINDOW)
    tok_n = lax.broadcasted_iota(jnp.int32, (rows, LANES), 0) % dec_seq
    col_n = lax.broadcasted_iota(jnp.int32, (rows, LANES), 1)
    valid_n = (col_n <= tok_n) & (col_n < dec_seq)
    lane_o = lax.broadcasted_iota(jnp.int32, (rows, LANES), 1)
    for i in range(nb):
        x = q_ref[i]
        kw = kw_ref[i].astype(BF16)
        vw = vw_ref[i].astype(BF16)
        kn = kn_ref[i].astype(BF16)
        vn = vn_ref[i].astype(BF16)
        outs = []
        for half in range(2):
            fm = lowf if half == 0 else ~lowf
            nm = lown if half == 0 else ~lown
            sw = jnp.where(valid_w, _nn(x, jnp.where(fm, kw, jnp.zeros_like(kw))), NEG_INF)
            sn = jnp.where(valid_n, _nt(x, jnp.where(nm, kn, jnp.zeros_like(kn))), NEG_INF)
            sink = sink_ref[:, half:half + 1]
            m = jnp.maximum(jnp.maximum(jnp.max(sw, axis=1, keepdims=True),
                                        jnp.max(sn, axis=1, keepdims=True)), sink)
            pw = jnp.exp(sw - m)
            pn = jnp.exp(sn - m)
            den = jnp.sum(pw, axis=1, keepdims=True) + jnp.sum(pn, axis=1, keepdims=True) + jnp.exp(sink - m)
            acc = _nt(pw.astype(BF16), vw) + _nn(pn.astype(BF16), vn)
            outs.append(acc / den)
        o_ref[i] = jnp.where(lane_o < HEAD_DIM, outs[0], outs[1]).astype(o_ref.dtype)


def _swa_sample(qx, kwt, knew, vwt, vnew, sinkmat, dec_seq, nb=8):
    bd, rows, _ = qx.shape
    w = kwt.shape[2]
    blk = lambda s: pl.BlockSpec((nb,) + s, lambda i: (i, 0, 0))
    return pl.pallas_call(
        functools.partial(_swa_sample_kernel, nb=nb, dec_seq=dec_seq),
        grid=(bd // nb,),
        in_specs=[blk((rows, LANES)), blk((LANES, w)), blk((LANES, LANES)), blk((LANES, w)), blk((LANES, LANES)),
                  pl.BlockSpec((rows, LANES), lambda i: (0, 0))],
        out_specs=blk((rows, LANES)),
        out_shape=jax.ShapeDtypeStruct((bd, rows, LANES), F32),
        compiler_params=_cparams(("parallel",)),
        name="swa_sample",
    )(qx, kwt, knew, vwt, vnew, sinkmat)


def _layer_norm(y, g, b):
    mu = jnp.mean(y, axis=1, keepdims=True)
    d = y - mu
    var = jnp.mean(d * d, axis=1, keepdims=True)
    return d * lax.rsqrt(var + LN_EPS) * g + b


def _out_ln_router_kernel(o_ref, x_ref, wo_ref, g_ref, b_ref, rh_ref, rl_ref, rb_ref, y_ref, comb_ref):
    y = _layer_norm(DEEPNORM_ALPHA * x_ref[...] + _nn(o_ref[...], wo_ref[...]), g_ref[...], b_ref[...])
    y_ref[...] = y
    yh = y.astype(BF16)
    yl = (y - yh.astype(F32)).astype(BF16)
    logits = _nn(yh, rh_ref[...]) + (_nn(yl, rh_ref[...]) + _nn(yh, rl_ref[...])) + rb_ref[...]
    lane = lax.broadcasted_iota(jnp.int32, logits.shape, 1)
    is_g = (lane >= N_EXPERTS) & (lane < N_EXPERTS + N_GROUPS)
    gl = jnp.where(is_g, logits, NEG_INF)
    gmax = jnp.max(gl, axis=1, keepdims=True)
    gsel = jnp.min(jnp.where(gl == gmax, lane, 2 * LANES), axis=1, keepdims=True) - N_EXPERTS
    gprob = 1.0 / jnp.sum(jnp.exp(gl - gmax), axis=1, keepdims=True)
    in_g = (lane >= gsel * EXPERTS_PER_GROUP) & (lane < (gsel + 1) * EXPERTS_PER_GROUP)
    el = jnp.where(in_g, logits, NEG_INF)
    v1 = jnp.max(el, axis=1, keepdims=True)
    i1 = jnp.min(jnp.where(el == v1, lane, 2 * LANES), axis=1, keepdims=True)
    el2 = jnp.where(lane == i1, NEG_INF, el)
    v2 = jnp.max(el2, axis=1, keepdims=True)
    i2 = jnp.min(jnp.where(el2 == v2, lane, 2 * LANES), axis=1, keepdims=True)
    e2 = jnp.exp(v2 - v1)
    den = 1.0 + e2
    comb_ref[...] = jnp.where(lane == i1, (1.0 / den) * gprob, jnp.where(lane == i2, (e2 / den) * gprob, 0.0))


def _out_ln_router(o, x, wo, g, b, rh, rl, rb, tile):
    n = x.shape[0]
    row = lambda i: (i, 0)
    fix = lambda i: (0, 0)
    return pl.pallas_call(
        _out_ln_router_kernel,
        grid=(n // tile,),
        in_specs=[pl.BlockSpec((tile, D_MODEL), row), pl.BlockSpec((tile, D_MODEL), row),
                  pl.BlockSpec((D_MODEL, D_MODEL), fix), pl.BlockSpec((1, D_MODEL), fix),
                  pl.BlockSpec((1, D_MODEL), fix), pl.BlockSpec((D_MODEL, LANES), fix),
                  pl.BlockSpec((D_MODEL, LANES), fix), pl.BlockSpec((1, LANES), fix)],
        out_specs=[pl.BlockSpec((tile, D_MODEL), row), pl.BlockSpec((tile, LANES), row)],
        out_shape=[jax.ShapeDtypeStruct((n, D_MODEL), F32), jax.ShapeDtypeStruct((n, LANES), F32)],
        compiler_params=_cparams(("parallel",)),
        name="out_ln_router",
    )(o, x, wo, g, b, rh, rl, rb)


def _moe_kernel(x_ref, comb_ref, wgu_ref, wdn_ref, g_ref, b_ref, y_ref, xb_ref, acc_ref, *, epb):
    j = pl.program_id(1)

    @pl.when(j == 0)
    def _():
        xb_ref[...] = x_ref[...].astype(BF16)
        acc_ref[...] = jnp.zeros_like(acc_ref)

    xb = xb_ref[...]
    comb = comb_ref[...]
    lane = lax.broadcasted_iota(jnp.int32, comb.shape, 1)
    for i in range(epb):
        e = j * epb + i
        gu = _nn(xb, wgu_ref[i])
        gate, up = gu[:, :D_FF_EXPERT], gu[:, D_FF_EXPERT:]
        w = jnp.sum(jnp.where(lane == e, comb, 0.0), axis=1, keepdims=True)
        h = (gate / (1.0 + jnp.exp(-gate))) * up * w
        acc_ref[...] += _nn(h.astype(BF16), wdn_ref[i])

    @pl.when(j == pl.num_programs(1) - 1)
    def _():
        y_ref[...] = _layer_norm(DEEPNORM_ALPHA * x_ref[...] + acc_ref[...], g_ref[...], b_ref[...])


def _moe(x, comb, wgu, wdn, g, b, tile, epb=4):
    n = x.shape[0]
    row = lambda i, j: (i, 0)
    fix = lambda i, j: (0, 0)
    return pl.pallas_call(
        functools.partial(_moe_kernel, epb=epb),
        grid=(n // tile, N_EXPERTS // epb),
        in_specs=[pl.BlockSpec((tile, D_MODEL), row), pl.BlockSpec((tile, LANES), row),
                  pl.BlockSpec((epb, D_MODEL, 2 * D_FF_EXPERT), lambda i, j: (j, 0, 0)),
                  pl.BlockSpec((epb, D_FF_EXPERT, D_MODEL), lambda i, j: (j, 0, 0)),
                  pl.BlockSpec((1, D_MODEL), fix), pl.BlockSpec((1, D_MODEL), fix)],
        out_specs=pl.BlockSpec((tile, D_MODEL), row),
        out_shape=jax.ShapeDtypeStruct((n, D_MODEL), F32),
        scratch_shapes=[pltpu.VMEM((tile, D_MODEL), BF16), pltpu.VMEM((tile, D_MODEL), F32)],
        compiler_params=_cparams(("parallel", "arbitrary")),
        name="moe",
    )(x, comb, wgu, wdn, g, b)


def _reduce0(x, op):
    r, w = x.shape
    g = next(g for g in (64, 32, 16, 8) if r % g == 0)
    return op(op(x.reshape(r // g, g, w), axis=0), axis=0, keepdims=True)


def _count(mask, axis):
    ones = jnp.where(mask, 1.0, 0.0)
    if axis == 0:
        return _reduce0(ones, jnp.sum)
    return jnp.sum(ones, axis=axis, keepdims=True)


def _order_key(x):
    bits = lax.bitcast_convert_type(x, jnp.int32)
    return jnp.where(bits < 0, bits ^ 0x7FFFFFFF, bits)


def _topk_mask(load_key, idx, k, axis, idx_bits):
    kf = float(k)
    r = jnp.where(_count(load_key() >= 0, axis) >= kf, 0, INT_MIN).astype(jnp.int32)

    def value_bit(i, r):
        cand = r | jnp.left_shift(jnp.int32(1), 30 - i)
        return jnp.where(_count(load_key() >= cand, axis) >= kf, cand, r)

    r = lax.fori_loop(0, 31, value_bit, r)
    need = kf - _count(load_key() > r, axis)

    def index_bit(i, j):
        cand = j | jnp.left_shift(jnp.int32(1), idx_bits - 1 - i)
        return jnp.where(_count((load_key() == r) & (idx < cand), axis) < need, cand, j)

    j = lax.fori_loop(0, idx_bits, index_bit, jnp.zeros_like(r))
    key = load_key()
    return (key > r) | ((key == r) & (idx <= j))


def _dsa_prompt_kernel(q_ref, kb_ref, vb_ref, kid_ref, qi_ref, wit_ref, o_ref, key_ref,
                       *, classes, topk, slot_kv):
    n = pl.program_id(1)
    for n0, cnt in classes:
        @pl.when((n >= n0) & (n < n0 + cnt))
        def _(n0=n0, cnt=cnt):
            _dsa_prompt_block(q_ref, kb_ref, vb_ref, kid_ref, qi_ref, wit_ref, o_ref, key_ref,
                              sk=(n0 + cnt) * BLOCK, topk=topk, slot_kv=slot_kv)


def _dsa_prompt_block(q_ref, kb_ref, vb_ref, kid_ref, qi_ref, wit_ref, o_ref, key_ref, *, sk, topk, slot_kv):
    t0 = pl.program_id(1) * BLOCK
    lane = lax.broadcasted_iota(jnp.int32, (BLOCK, LANES), 1)
    low = lane < HEAD_DIM
    s_idx = lax.broadcasted_iota(jnp.int32, (sk, BLOCK), 0)
    causal = s_idx <= t0 + lax.broadcasted_iota(jnp.int32, (sk, BLOCK), 1)

    kid = kid_ref[0:sk, :]
    sc = jnp.zeros((sk, BLOCK), F32)
    for hp in range(IDX_HEADS // 2):
        slab = qi_ref[:, hp * LANES:(hp + 1) * LANES]
        z = jnp.zeros_like(slab)
        qh2 = jnp.concatenate([jnp.where(low, slab, z), jnp.where(low, z, slab)], axis=0)
        d = jnp.maximum(_nt(kid, qh2), 0.0)
        sc = sc + d[:, :BLOCK] * wit_ref[2 * hp:2 * hp + 1, :] + d[:, BLOCK:] * wit_ref[2 * hp + 1:2 * hp + 2, :]
    key_ref[0:sk, :] = _order_key(jnp.where(causal, sc, NEG_INF))
    sel = _topk_mask(lambda: key_ref[0:sk, :], s_idx, topk, 0, (sk - 1).bit_length()) & causal
    bias = jnp.where(sel, 0.0, NEG_INF).T

    kb = kb_ref[0:sk, :]
    vb = vb_ref[0:sk, :]
    res = [None] * N_HEADS
    for kv in range(KV_HEADS_B):
        slots = [p for p in range(N_HEADS) if slot_kv[p] == kv]
        qes = []
        for p in slots:
            slab = q_ref[:, (p // 2) * LANES:(p // 2 + 1) * LANES]
            z = jnp.zeros_like(slab)
            qm = jnp.where(low, slab, z) if p % 2 == 0 else jnp.where(low, z, slab)
            qes.append(jnp.concatenate([qm, z] if kv // 2 == 0 else [z, qm], axis=1))
        s_all = _nt(jnp.concatenate(qes, axis=0), kb)
        prs, dens = [], []
        for i in range(len(slots)):
            s = s_all[i * BLOCK:(i + 1) * BLOCK, :] + bias
            pr = jnp.exp2(s - jnp.max(s, axis=1, keepdims=True))
            dens.append(jnp.sum(pr, axis=1, keepdims=True))
            prs.append(pr.astype(BF16))
        o_all = _nn(jnp.concatenate(prs, axis=0), vb)
        for i, p in enumerate(slots):
            res[p] = o_all[i * BLOCK:(i + 1) * BLOCK, (kv // 2) * LANES:(kv // 2 + 1) * LANES] / dens[i]
    for pair in range(N_HEADS // 2):
        o_ref[:, pair * LANES:(pair + 1) * LANES] = jnp.where(low, res[2 * pair], res[2 * pair + 1]).astype(o_ref.dtype)


def _dsa_prompt(q, kb, vb, kid, qi, wit, batch, seq, topk, slot_kv, n_classes=2):
    nb = seq // BLOCK
    per = -(-nb // n_classes)
    classes = tuple((n0, min(per, nb - n0)) for n0 in range(0, nb, per))
    qmap = lambda b, n: (b * nb + n, 0)
    return pl.pallas_call(
        functools.partial(_dsa_prompt_kernel, classes=classes, topk=topk, slot_kv=slot_kv),
        grid=(batch, nb),
        in_specs=[pl.BlockSpec((BLOCK, D_MODEL), qmap),
                  pl.BlockSpec((seq, 2 * LANES), lambda b, n: (b, 0)),
                  pl.BlockSpec((seq, 2 * LANES), lambda b, n: (b, 0)),
                  pl.BlockSpec((seq, LANES), lambda b, n: (b, 0)),
                  pl.BlockSpec((BLOCK, IDX_HEADS * IDX_DIM), qmap),
                  pl.BlockSpec((IDX_HEADS, BLOCK), lambda b, n: (0, b * nb + n))],
        out_specs=pl.BlockSpec((BLOCK, D_MODEL), qmap),
        out_shape=jax.ShapeDtypeStruct((batch * seq, D_MODEL), BF16),
        scratch_shapes=[pltpu.VMEM((seq, BLOCK), jnp.int32)],
        compiler_params=_cparams(("parallel", "arbitrary")),
        name="dsa_prompt",
    )(q, kb, vb, kid, qi, wit)


def _page_copies(pt_ref, pool_ref, buf_ref, sem, step, slot, pairb, page0, n_pages):
    cps = []
    for bi in range(pairb):
        for i in range(n_pages):
            page = pt_ref[step * pairb + bi, page0 + i]
            cps.append(pltpu.make_async_copy(
                pool_ref.at[page], buf_ref.at[slot, bi, :, pl.ds(i * PAGE_SIZE, PAGE_SIZE)], sem.at[slot]))
    return cps


def _dsa_sample_scores_kernel(pt_ref, pool_ref, qi_ref, w_ref, kin_ref, o_ref, buf_ref, sem, *, pairb, n_pages, dec_seq):
    g = pl.program_id(0)
    ng = pl.num_programs(0)
    slot = g % 2

    @pl.when(g == 0)
    def _():
        for cp in _page_copies(pt_ref, pool_ref, buf_ref, sem, 0, 0, pairb, 0, n_pages):
            cp.start()

    @pl.when(g + 1 < ng)
    def _():
        for cp in _page_copies(pt_ref, pool_ref, buf_ref, sem, g + 1, 1 - slot, pairb, 0, n_pages):
            cp.start()

    for cp in _page_copies(pt_ref, pool_ref, buf_ref, sem, g, slot, pairb, 0, n_pages):
        cp.wait()

    past = n_pages * PAGE_SIZE
    lane = lax.broadcasted_iota(jnp.int32, (1, LANES), 1)
    for bi in range(pairb):
        qi = qi_ref[bi]
        w = w_ref[bi]
        r = jnp.maximum(_nn(qi, buf_ref[slot, bi].astype(BF16)), 0.0) * w
        rn = jnp.maximum(_nt(qi, kin_ref[bi]), 0.0) * w
        for t in range(dec_seq):
            row = bi * dec_seq + t
            o_ref[row:row + 1, 0:past] = jnp.sum(r[t * IDX_HEADS:(t + 1) * IDX_HEADS], axis=0, keepdims=True)
            new = jnp.sum(rn[t * IDX_HEADS:(t + 1) * IDX_HEADS], axis=0, keepdims=True)
            o_ref[row:row + 1, past:past + LANES] = jnp.where(lane <= t, new, NEG_INF)


def _dsa_sample_scores(page_table, pool_t, qix, wcol, kin, dec_seq, pairb=2):
    bd, n_pages = page_table.shape
    past = n_pages * PAGE_SIZE
    rows = dec_seq * IDX_HEADS
    blk = lambda s: pl.BlockSpec((pairb,) + s, lambda g, pt: (g, 0, 0))
    return pl.pallas_call(
        functools.partial(_dsa_sample_scores_kernel, pairb=pairb, n_pages=n_pages, dec_seq=dec_seq),
        grid_spec=pltpu.PrefetchScalarGridSpec(
            num_scalar_prefetch=1, grid=(bd // pairb,),
            in_specs=[pl.BlockSpec(memory_space=pl.ANY), blk((rows, IDX_DIM)), blk((rows, 1)), blk((LANES, IDX_DIM))],
            out_specs=pl.BlockSpec((pairb * dec_seq, past + LANES), lambda g, pt: (g, 0)),
            scratch_shapes=[pltpu.VMEM((2, pairb, IDX_DIM, past), F32), pltpu.SemaphoreType.DMA((2,))]),
        out_shape=jax.ShapeDtypeStruct((bd * dec_seq, past + LANES), F32),
        compiler_params=_cparams(("arbitrary",)),
        name="dsa_sample_scores",
    )(page_table, pool_t, qix, wcol, kin)


def _select_bias_kernel(sc_ref, o_ref, key_ref, *, topk):
    idx = lax.broadcasted_iota(jnp.int32, sc_ref.shape, 1)
    key_ref[...] = _order_key(sc_ref[...])
    sel = _topk_mask(lambda: key_ref[...], idx, topk, 1, (sc_ref.shape[1] - 1).bit_length())
    o_ref[...] = jnp.where(sel & (sc_ref[...] > NEG_INF), 0.0, NEG_INF)


def _select_bias(sc, topk, tile):
    r, c = sc.shape
    return pl.pallas_call(
        functools.partial(_select_bias_kernel, topk=topk),
        grid=(r // tile,),
        in_specs=[pl.BlockSpec((tile, c), lambda i: (i, 0))],
        out_specs=pl.BlockSpec((tile, c), lambda i: (i, 0)),
        out_shape=jax.ShapeDtypeStruct((r, c), F32),
        scratch_shapes=[pltpu.VMEM((tile, c), jnp.int32)],
        compiler_params=_cparams(("parallel",)),
        name="select_bias",
    )(sc)


def _dsa_sample_attn_kernel(pt_ref, kpool_ref, vpool_ref, qe_ref, bias_ref, biasn_ref, kn_ref, vn_ref, o_ref,
                            kbuf, vbuf, ksem, vsem, m_ref, l_ref, acc_ref, *, pairb, chunk_pages, dec_seq):
    g = pl.program_id(0)
    c = pl.program_id(1)
    nch = pl.num_programs(1)
    lin = g * nch + c
    total = pl.num_programs(0) * nch
    slot = lin % 2

    def copies(step_lin, slot_):
        gg = step_lin // nch
        cc = step_lin % nch
        return (_page_copies(pt_ref, kpool_ref, kbuf, ksem, gg, slot_, pairb, cc * chunk_pages, chunk_pages)
                + _page_copies(pt_ref, vpool_ref, vbuf, vsem, gg, slot_, pairb, cc * chunk_pages, chunk_pages))

    @pl.when(lin == 0)
    def _():
        for cp in copies(0, 0):
            cp.start()

    @pl.when(lin + 1 < total)
    def _():
        for cp in copies(lin + 1, 1 - slot):
            cp.start()

    for cp in copies(lin, slot):
        cp.wait()

    @pl.when(c == 0)
    def _():
        m_ref[...] = jnp.full_like(m_ref, -1e30)
        l_ref[...] = jnp.zeros_like(l_ref)
        acc_ref[...] = jnp.zeros_like(acc_ref)

    rows = qe_ref.shape[1]
    per_tok = rows // dec_seq

    def expand(b4):
        return jnp.concatenate(
            [jnp.broadcast_to(b4[t:t + 1, :], (per_tok, b4.shape[1])) for t in range(dec_seq)], axis=0)

    def update(bi, s, pv):
        m_old = m_ref[bi]
        m_new = jnp.maximum(m_old, jnp.max(s, axis=1, keepdims=True))
        alpha = jnp.exp2(m_old - m_new)
        pr = jnp.exp2(s - m_new)
        l_ref[bi] = alpha * l_ref[bi] + jnp.sum(pr, axis=1, keepdims=True)
        acc_ref[bi] = alpha * acc_ref[bi] + pv(pr.astype(BF16))
        m_ref[bi] = m_new

    for bi in range(pairb):
        qe = qe_ref[bi]
        kt = kbuf[slot, bi].astype(BF16)
        vt = vbuf[slot, bi].astype(BF16)
        s = _nn(qe, kt) + expand(bias_ref[bi * dec_seq:(bi + 1) * dec_seq, :])
        update(bi, s, lambda pr: _nt(pr, vt))

    @pl.when(c == nch - 1)
    def _():
        for bi in range(pairb):
            qe = qe_ref[bi]
            s = _nt(qe, kn_ref[bi]) + expand(biasn_ref[bi * dec_seq:(bi + 1) * dec_seq, :])
            update(bi, s, lambda pr: _nn(pr, vn_ref[bi]))
            o_ref[bi] = acc_ref[bi] / l_ref[bi]


def _dsa_sample_attn(page_table, kpool_t, vpool_t, qe, bias, knew, vnew, dec_seq, pairb=2, chunk_pages=16):
    bd, n_pages = page_table.shape
    nch = n_pages // chunk_pages
    chunk = chunk_pages * PAGE_SIZE
    rows = qe.shape[1]
    kvw = qe.shape[2]
    blk = lambda s: pl.BlockSpec((pairb,) + s, lambda g, c, pt: (g, 0, 0))
    return pl.pallas_call(
        functools.partial(_dsa_sample_attn_kernel, pairb=pairb, chunk_pages=chunk_pages, dec_seq=dec_seq),
        grid_spec=pltpu.PrefetchScalarGridSpec(
            num_scalar_prefetch=1, grid=(bd // pairb, nch),
            in_specs=[pl.BlockSpec(memory_space=pl.ANY), pl.BlockSpec(memory_space=pl.ANY),
                      blk((rows, kvw)),
                      pl.BlockSpec((pairb * dec_seq, chunk), lambda g, c, pt: (g, c)),
                      pl.BlockSpec((pairb * dec_seq, LANES), lambda g, c, pt: (g, n_pages)),
                      blk((LANES, kvw)), blk((LANES, kvw))],
            out_specs=blk((rows, kvw)),
            scratch_shapes=[pltpu.VMEM((2, pairb, kvw, chunk), F32), pltpu.VMEM((2, pairb, kvw, chunk), F32),
                            pltpu.SemaphoreType.DMA((2,)), pltpu.SemaphoreType.DMA((2,)),
                            pltpu.VMEM((pairb, rows, 1), F32), pltpu.VMEM((pairb, rows, 1), F32),
                            pltpu.VMEM((pairb, rows, kvw), F32)]),
        out_shape=jax.ShapeDtypeStruct((bd, rows, kvw), F32),
        compiler_params=_cparams(("arbitrary", "arbitrary")),
        name="dsa_sample_attn",
    )(page_table, kpool_t, vpool_t, qe, bias, bias, knew, vnew)


def _pick_tile(n, candidates):
    for t in candidates:
        if n % t == 0:
            return t
    raise ValueError(f"no token tile for {n}")


def _hi_lo(w):
    hi = w.astype(BF16)
    return hi, (w - hi.astype(F32)).astype(BF16)


def _pad_rows(x, rows):
    return jnp.pad(x, ((0, 0), (0, rows - x.shape[1]), (0, 0)))


def _post_attention(o, x, w_o_phys, l, moe, ln, tile, moe_tile):
    (w_rg, b_rg, w_re, b_re, w_gu, w_dn) = moe
    (ln1_g, ln1_b, ln2_g, ln2_b) = ln
    rw = jnp.zeros((D_MODEL, LANES), F32).at[:, :N_EXPERTS].set(w_re[l])
    rw = rw.at[:, N_EXPERTS:N_EXPERTS + N_GROUPS].set(w_rg[l])
    rb = jnp.zeros((1, LANES), F32).at[0, :N_EXPERTS].set(b_re[l]).at[0, N_EXPERTS:N_EXPERTS + N_GROUPS].set(b_rg[l])
    rh, rl = _hi_lo(rw)
    y, comb = _out_ln_router(o, x, w_o_phys.astype(BF16), ln1_g[l].reshape(1, -1), ln1_b[l].reshape(1, -1),
                             rh, rl, rb, tile)
    return _moe(y, comb, w_gu[l].astype(BF16), w_dn[l].astype(BF16),
                ln2_g[l].reshape(1, -1), ln2_b[l].reshape(1, -1), moe_tile)


def kernel(x_prompt, x_sample, cache_win_k, cache_win_v, cache_k, cache_v, cache_idx_k, page_table,
           a_w_qkv, a_b_qkv, a_sinks, a_w_o, b_w_in, b_w_o,
           moe_w_rg, moe_b_rg, moe_w_re, moe_b_re, moe_w_gu, moe_w_dn,
           ln1_g, ln1_b, ln2_g, ln2_b):
    B, S, _ = x_prompt.shape
    Bd, T, _ = x_sample.shape
    n_p, n_s = B * S, Bd * T
    n_tot = n_p + n_s
    past = page_table.shape[1] * PAGE_SIZE
    tile = _pick_tile(n_tot, (512, 256, 128, 64, 32, 16, 8))
    moe_tile = tile
    qd = N_HEADS * HEAD_DIM
    moe = (moe_w_rg, moe_b_rg, moe_w_re, moe_b_re, moe_w_gu, moe_w_dn)
    ln = (ln1_g, ln1_b, ln2_g, ln2_b)

    x = jnp.concatenate([x_prompt.reshape(n_p, D_MODEL), x_sample.reshape(n_s, D_MODEL)], axis=0)
    pos = jnp.concatenate([jnp.tile(jnp.arange(S, dtype=jnp.int32), B),
                           jnp.tile(past + jnp.arange(T, dtype=jnp.int32), Bd)])
    cos, sin = _rope_tables(pos)
    scale = HEAD_DIM ** -0.5

    a = 0
    kva = KV_HEADS_A * HEAD_DIM
    order_a = _slot_order(KV_HEADS_A)
    perm_a = _col_perm(order_a)
    wq, wk, wv = a_w_qkv[a][:, :qd], a_w_qkv[a][:, qd:qd + kva], a_w_qkv[a][:, qd + kva:]
    bq, bk, bv = a_b_qkv[a][:qd], a_b_qkv[a][qd:qd + kva], a_b_qkv[a][qd + kva:]
    w_a = jnp.concatenate([wq[:, perm_a], wk, wv], axis=1).astype(BF16)
    b_a = jnp.concatenate([bq[perm_a], bk, bv]).reshape(1, -1)
    plan_a = ((0, 8, True, scale, (0,)), (8, 1, True, 1.0, (1, 2)), (9, 1, False, 1.0, (3, 4)))
    q0, k0, k0b, v0, v0b = _project(x, w_a, b_a, cos, sin, plan_a,
                                    [(qd, BF16), (kva, F32), (kva, BF16), (kva, F32), (kva, BF16)], tile)
    sinks_phys = a_sinks[a][np.asarray(order_a)]
    o_p = _swa_prompt(q0, k0b, v0b, sinks_phys, B, S)

    w_win = cache_win_k.shape[2]
    qx = q0[n_p:].reshape(Bd, T, N_HEADS // 2, LANES).transpose(0, 2, 1, 3).reshape(Bd, T * N_HEADS // 2, LANES)
    kwt = cache_win_k[a].transpose(0, 2, 3, 1).reshape(Bd, kva, w_win)
    vwt = cache_win_v[a].transpose(0, 2, 3, 1).reshape(Bd, kva, w_win)
    k0s = k0[n_p:].reshape(Bd, T, kva)
    v0s = v0[n_p:].reshape(Bd, T, kva)
    sinkmat = jnp.zeros((T * N_HEADS // 2, LANES), F32).at[:, :2].set(
        jnp.repeat(sinks_phys.reshape(N_HEADS // 2, 2), T, axis=0))
    ox = _swa_sample(qx, kwt, _pad_rows(k0s, LANES), vwt, _pad_rows(v0s, LANES), sinkmat, T,
                     nb=_pick_tile(Bd, (8, 4, 2, 1)))
    o_s = ox.reshape(Bd, N_HEADS // 2, T, LANES).transpose(0, 2, 1, 3).reshape(n_s, qd).astype(BF16)

    x = _post_attention(jnp.concatenate([o_p, o_s], axis=0), x, a_w_o[a][perm_a, :], 0, moe, ln, tile, moe_tile)

    wp = min(WINDOW, S)
    win_k_prompt = k0[:n_p].reshape(B, S, KV_HEADS_A, HEAD_DIM)[:, S - wp:][None]
    win_v_prompt = v0[:n_p].reshape(B, S, KV_HEADS_A, HEAD_DIM)[:, S - wp:][None]
    win_k_sample = jnp.concatenate([cache_win_k[a], k0s.reshape(Bd, T, KV_HEADS_A, HEAD_DIM)], axis=1)[:, -w_win:][None]
    win_v_sample = jnp.concatenate([cache_win_v[a], v0s.reshape(Bd, T, KV_HEADS_A, HEAD_DIM)], axis=1)[:, -w_win:][None]

    bl = 0
    kvb = KV_HEADS_B * HEAD_DIM
    qid = IDX_HEADS * IDX_DIM
    order_b = _slot_order(KV_HEADS_B)
    perm_b = _col_perm(order_b)
    slot_kv = tuple(h // (N_HEADS // KV_HEADS_B) for h in order_b)
    w_in = b_w_in[bl]
    c0 = 0
    wq = w_in[:, c0:c0 + qd]; c0 += qd
    wk = w_in[:, c0:c0 + kvb]; c0 += kvb
    wv = w_in[:, c0:c0 + kvb]; c0 += kvb
    wqi = w_in[:, c0:c0 + qid]; c0 += qid
    wki = w_in[:, c0:c0 + IDX_DIM]; c0 += IDX_DIM
    wwi = w_in[:, c0:c0 + IDX_HEADS]
    w_b = jnp.concatenate([wq[:, perm_b], wk, wv, wqi, wki, wki], axis=1).astype(BF16)
    b_b = jnp.zeros((1, w_b.shape[1]), F32)
    plan_b = ((0, 8, True, scale * math.log2(math.e), (0,)), (8, 2, True, 1.0, (1, 2)), (10, 2, False, 1.0, (3, 7)),
              (12, 4, True, IDX_DIM ** -0.5, (4,)), (16, 1, True, 1.0, (5, 6)))
    q1, k1, k1b, v1, qi1, ki1, kid1, v1b = _project(
        x, w_b, b_b, cos, sin, plan_b,
        [(qd, BF16), (kvb, F32), (kvb, BF16), (kvb, F32), (qid, BF16), (LANES, F32), (LANES, BF16), (kvb, BF16)],
        tile)
    wwi_t = jnp.zeros((2 * IDX_HEADS, D_MODEL), F32).at[:IDX_HEADS].set(wwi.T).astype(BF16)
    wit1 = _project_t(x, wwi_t, IDX_HEADS ** -0.5, F32, tile)

    o_p = _dsa_prompt(q1, k1b, v1b, kid1, qi1, wit1, B, S, min(TOPK_MAX, S // 4), slot_kv)

    n_pool = cache_k.shape[1]
    pool_ik_t = cache_idx_k[bl].transpose(0, 2, 1)
    qix = qi1[n_p:].reshape(Bd, T * IDX_HEADS, IDX_DIM)
    wcol = wit1[:IDX_HEADS, n_p:].T.reshape(Bd, T * IDX_HEADS, 1)
    kin = _pad_rows(kid1[n_p:, :IDX_DIM].reshape(Bd, T, IDX_DIM), LANES)
    sc = _dsa_sample_scores(page_table, pool_ik_t, qix, wcol, kin, T)
    bias = _select_bias(sc, min(TOPK_MAX, (past + T) // 4), _pick_tile(n_s, (128, 64, 32, 16, 8)))
    kpool_t = cache_k[bl].transpose(0, 2, 3, 1).reshape(n_pool, kvb, PAGE_SIZE)
    vpool_t = cache_v[bl].transpose(0, 2, 3, 1).reshape(n_pool, kvb, PAGE_SIZE)
    onehot = (np.asarray(slot_kv)[:, None] == np.arange(KV_HEADS_B)[None, :]).astype(np.float32)
    qe = (q1[n_p:].reshape(n_s, N_HEADS, 1, HEAD_DIM) * jnp.asarray(onehot, BF16)[None, :, :, None])
    qe = qe.reshape(Bd, T * N_HEADS, kvb)
    knew = _pad_rows(k1b[n_p:].reshape(Bd, T, kvb), LANES)
    vnew = _pad_rows(v1b[n_p:].reshape(Bd, T, kvb), LANES)
    ox = _dsa_sample_attn(page_table, kpool_t, vpool_t, qe, bias, knew, vnew, T,
                          chunk_pages=_pick_tile(page_table.shape[1], (16, 8, 4, 2, 1)))
    ox = ox.reshape(Bd, T, N_HEADS, KV_HEADS_B, HEAD_DIM)
    o_s = ox[:, :, np.arange(N_HEADS), np.asarray(slot_kv), :].reshape(n_s, qd).astype(BF16)

    x = _post_attention(jnp.concatenate([o_p, o_s], axis=0), x, b_w_o[bl][perm_b, :], 1, moe, ln, tile, moe_tile)

    k_prompt = k1[:n_p].reshape(B, S, KV_HEADS_B, HEAD_DIM)[None]
    v_prompt = v1[:n_p].reshape(B, S, KV_HEADS_B, HEAD_DIM)[None]
    idx_k_prompt = ki1[:n_p, :IDX_DIM].reshape(B, S, IDX_DIM)[None]
    k_sample = k1[n_p:].reshape(Bd, T, KV_HEADS_B, HEAD_DIM)[None]
    v_sample = v1[n_p:].reshape(Bd, T, KV_HEADS_B, HEAD_DIM)[None]
    idx_k_sample = ki1[n_p:, :IDX_DIM].reshape(Bd, T, IDX_DIM)[None]

    return (x[:n_p].reshape(B, S, D_MODEL), x[n_p:].reshape(Bd, T, D_MODEL),
            win_k_prompt, win_v_prompt, win_k_sample, win_v_sample,
            k_prompt, v_prompt, idx_k_prompt, k_sample, v_sample, idx_k_sample)
```

```python
import functools
import math

import jax
import jax.numpy as jnp
import numpy as np
from jax import lax
from jax.experimental import pallas as pl
from jax.experimental.pallas import tpu as pltpu

D_MODEL = 1024
HEAD_DIM = 64
N_HEADS = 16
KV_HEADS_A = 2
KV_HEADS_B = 4
WINDOW = 128
BLOCK = 128
IDX_HEADS = 8
IDX_DIM = 64
TOPK_MAX = 256
N_GROUPS = 4
EXPERTS_PER_GROUP = 8
N_EXPERTS = N_GROUPS * EXPERTS_PER_GROUP
D_FF_EXPERT = 256
ROPE_THETA = 10000.0
LN_EPS = 1e-5
DEPTH = 2
DEEPNORM_ALPHA = (2 * DEPTH) ** 0.25
PAST_LEN = 8192
PAGE_SIZE = 128

LANES = 128
NEG_INF = float("-inf")
INT_MIN = -(2 ** 31)
VMEM_LIMIT = 56 * 1024 * 1024

F32 = jnp.float32
BF16 = jnp.bfloat16


def _slot_order(kv_heads):
    g = N_HEADS // kv_heads
    even = [h for h in range(N_HEADS) if (h // g) % 2 == 0]
    odd = [h for h in range(N_HEADS) if (h // g) % 2 == 1]
    order = []
    for a, b in zip(even, odd):
        order += [a, b]
    return order


def _col_perm(order):
    return np.concatenate([np.arange(h * HEAD_DIM, (h + 1) * HEAD_DIM) for h in order])


def _rope_tables(pos):
    half = HEAD_DIM // 2
    inv = jnp.exp(jnp.arange(half, dtype=F32) * (-2.0 * math.log(ROPE_THETA) / HEAD_DIM))
    ang = pos.astype(F32)[:, None] * inv[None, :]
    c, s = jnp.cos(ang), jnp.sin(ang)
    return jnp.concatenate([c, c, c, c], axis=1), jnp.concatenate([-s, s, -s, s], axis=1)


def _rope_slab(x, cos, sin, lane):
    swapped = jnp.where((lane % HEAD_DIM) < HEAD_DIM // 2,
                        pltpu.roll(x, LANES - HEAD_DIM // 2, axis=1),
                        pltpu.roll(x, HEAD_DIM // 2, axis=1))
    return x * cos + swapped * sin


def _nt(a, b):
    return lax.dot_general(a, b, (((1,), (1,)), ((), ())), preferred_element_type=F32)


def _nn(a, b):
    return jnp.dot(a, b, preferred_element_type=F32)


def _cparams(sem):
    return pltpu.CompilerParams(dimension_semantics=sem, vmem_limit_bytes=VMEM_LIMIT)


def _proj_kernel(x_ref, w_ref, b_ref, cos_ref, sin_ref, *out_refs, plan):
    xb = x_ref[...].astype(BF16)
    cos = cos_ref[...]
    sin = sin_ref[...]
    lane = lax.broadcasted_iota(jnp.int32, (x_ref.shape[0], LANES), 1)
    for (s0, ns, rope, scale, ois) in plan:
        acc = _nn(xb, w_ref[:, s0 * LANES:(s0 + ns) * LANES]) + b_ref[:, s0 * LANES:(s0 + ns) * LANES]
        for j in range(ns):
            slab = acc[:, j * LANES:(j + 1) * LANES]
            if rope:
                slab = _rope_slab(slab, cos, sin, lane)
            if scale != 1.0:
                slab = slab * scale
            for oi in ois:
                o_ref = out_refs[oi]
                o_ref[:, j * LANES:(j + 1) * LANES] = slab.astype(o_ref.dtype)


def _project(x, w, b, cos, sin, plan, outs, tile):
    n, d = x.shape
    f = w.shape[1]
    out_shape = [jax.ShapeDtypeStruct((n, wd), dt) for wd, dt in outs]
    out_specs = [pl.BlockSpec((tile, wd), lambda i: (i, 0)) for wd, _ in outs]
    return pl.pallas_call(
        functools.partial(_proj_kernel, plan=plan),
        grid=(n // tile,),
        in_specs=[pl.BlockSpec((tile, d), lambda i: (i, 0)),
                  pl.BlockSpec((d, f), lambda i: (0, 0)),
                  pl.BlockSpec((1, f), lambda i: (0, 0)),
                  pl.BlockSpec((tile, LANES), lambda i: (i, 0)),
                  pl.BlockSpec((tile, LANES), lambda i: (i, 0))],
        out_specs=out_specs,
        out_shape=out_shape,
        compiler_params=_cparams(("parallel",)),
        name="proj",
    )(x, w, b, cos, sin)


def _projT_kernel(x_ref, wt_ref, o_ref, *, scale):
    acc = _nt(wt_ref[...], x_ref[...].astype(BF16))
    if scale != 1.0:
        acc = acc * scale
    o_ref[...] = acc.astype(o_ref.dtype)


def _project_t(x, wt, scale, dtype, tile):
    n, d = x.shape
    f = wt.shape[0]
    return pl.pallas_call(
        functools.partial(_projT_kernel, scale=scale),
        grid=(n // tile,),
        in_specs=[pl.BlockSpec((tile, d), lambda i: (i, 0)),
                  pl.BlockSpec((f, d), lambda i: (0, 0))],
        out_specs=pl.BlockSpec((f, tile), lambda i: (0, i)),
        out_shape=jax.ShapeDtypeStruct((f, n), dtype),
        compiler_params=_cparams(("parallel",)),
        name="proj_t",
    )(x, wt)


def _swa_prompt_kernel(sink_ref, q_ref, kp_ref, kc_ref, vp_ref, vc_ref, o_ref):
    n = pl.program_id(1)
    kband = jnp.concatenate([kp_ref[...], kc_ref[...]], axis=0)
    vband = jnp.concatenate([vp_ref[...], vc_ref[...]], axis=0)
    qi = lax.broadcasted_iota(jnp.int32, (BLOCK, 2 * BLOCK), 0)
    kj = lax.broadcasted_iota(jnp.int32, (BLOCK, 2 * BLOCK), 1)
    rel = qi - kj + BLOCK
    valid = (rel >= 0) & (rel < WINDOW) & ((n > 0) | (kj >= BLOCK))
    lane = lax.broadcasted_iota(jnp.int32, (BLOCK, LANES), 1)
    low = lane < HEAD_DIM
    for pair in range(N_HEADS // 2):
        slab = q_ref[:, pair * LANES:(pair + 1) * LANES]
        outs = []
        for half in range(2):
            p = 2 * pair + half
            qe = jnp.where(low if half == 0 else ~low, slab, jnp.zeros_like(slab))
            s = jnp.where(valid, _nt(qe, kband), NEG_INF)
            sink = sink_ref[p]
            m = jnp.maximum(jnp.max(s, axis=1, keepdims=True), sink)
            pr = jnp.exp(s - m)
            den = jnp.sum(pr, axis=1, keepdims=True) + jnp.exp(sink - m)
            outs.append(_nn(pr.astype(BF16), vband) / den)
        o_ref[:, pair * LANES:(pair + 1) * LANES] = jnp.where(low, outs[0], outs[1]).astype(o_ref.dtype)


def _swa_prompt(q, kb, vb, sinks, batch, seq):
    nb = seq // BLOCK
    cur = lambda b, n, s: (b * nb + n, 0)
    prev = lambda b, n, s: (b * nb + jnp.maximum(n - 1, 0), 0)
    return pl.pallas_call(
        _swa_prompt_kernel,
        grid_spec=pltpu.PrefetchScalarGridSpec(
            num_scalar_prefetch=1, grid=(batch, nb),
            in_specs=[pl.BlockSpec((BLOCK, D_MODEL), cur),
                      pl.BlockSpec((BLOCK, LANES), prev), pl.BlockSpec((BLOCK, LANES), cur),
                      pl.BlockSpec((BLOCK, LANES), prev), pl.BlockSpec((BLOCK, LANES), cur)],
            out_specs=pl.BlockSpec((BLOCK, D_MODEL), cur)),
        out_shape=jax.ShapeDtypeStruct((batch * seq, D_MODEL), BF16),
        compiler_params=_cparams(("parallel", "parallel")),
        name="swa_prompt",
    )(sinks, q, kb, kb, vb, vb)


def _swa_sample_kernel(q_ref, kw_ref, kn_ref, vw_ref, vn_ref, sink_ref, o_ref, *, nb, dec_seq):
    w = kw_ref.shape[2]
    rows = q_ref.shape[1]
    lane = lax.broadcasted_iota(jnp.int32, (LANES, w), 0)
    lowf = lane < HEAD_DIM
    lane_n = lax.broadcasted_iota(jnp.int32, (LANES, LANES), 1)
    lown = lane_n < HEAD_DIM
    tok = lax.broadcasted_iota(jnp.int32, (rows, w), 0) % dec_seq
    col = lax.broadcasted_iota(jnp.int32, (rows, w), 1)
    valid_w = col > tok + (w - WINDOW)
    tok_n = lax.broadcasted_iota(jnp.int32, (rows, LANES), 0) % dec_seq
    col_n = lax.broadcasted_iota(jnp.int32, (rows, LANES), 1)
    valid_n = (col_n <= tok_n) & (col_n < dec_seq)
    lane_o = lax.broadcasted_iota(jnp.int32, (rows, LANES), 1)
    for i in range(nb):
        x = q_ref[i]
        kw = kw_ref[i].astype(BF16)
        vw = vw_ref[i].astype(BF16)
        kn = kn_ref[i].astype(BF16)
        vn = vn_ref[i].astype(BF16)
        outs = []
        for half in range(2):
            fm = lowf if half == 0 else ~lowf
            nm = lown if half == 0 else ~lown
            sw = jnp.where(valid_w, _nn(x, jnp.where(fm, kw, jnp.zeros_like(kw))), NEG_INF)
            sn = jnp.where(valid_n, _nt(x, jnp.where(nm, kn, jnp.zeros_like(kn))), NEG_INF)
            sink = sink_ref[:, half:half + 1]
            m = jnp.maximum(jnp.maximum(jnp.max(sw, axis=1, keepdims=True),
                                        jnp.max(sn, axis=1, keepdims=True)), sink)
            pw = jnp.exp(sw - m)
            pn = jnp.exp(sn - m)
            den = jnp.sum(pw, axis=1, keepdims=True) + jnp.sum(pn, axis=1, keepdims=True) + jnp.exp(sink - m)
            acc = _nt(pw.astype(BF16), vw) + _nn(pn.astype(BF16), vn)
            outs.append(acc / den)
        o_ref[i] = jnp.where(lane_o < HEAD_DIM, outs[0], outs[1]).astype(o_ref.dtype)


def _swa_sample(qx, kwt, knew, vwt, vnew, sinkmat, dec_seq, nb=8):
    bd, rows, _ = qx.shape
    w = kwt.shape[2]
    blk = lambda s: pl.BlockSpec((nb,) + s, lambda i: (i, 0, 0))
    return pl.pallas_call(
        functools.partial(_swa_sample_kernel, nb=nb, dec_seq=dec_seq),
        grid=(bd // nb,),
        in_specs=[blk((rows, LANES)), blk((LANES, w)), blk((LANES, LANES)), blk((LANES, w)), blk((LANES, LANES)),
                  pl.BlockSpec((rows, LANES), lambda i: (0, 0))],
        out_specs=blk((rows, LANES)),
        out_shape=jax.ShapeDtypeStruct((bd, rows, LANES), F32),
        compiler_params=_cparams(("parallel",)),
        name="swa_sample",
    )(qx, kwt, knew, vwt, vnew, sinkmat)


def _layer_norm(y, g, b):
    mu = jnp.mean(y, axis=1, keepdims=True)
    d = y - mu
    var = jnp.mean(d * d, axis=1, keepdims=True)
    return d * lax.rsqrt(var + LN_EPS) * g + b


ROUTE_E1, ROUTE_E2, ROUTE_G1, ROUTE_G2, ROUTE_R1, ROUTE_R2 = range(6)


def _out_ln_router_kernel(o_ref, x_ref, wo_ref, g_ref, b_ref, rh_ref, rl_ref, rb_ref, tri_ref,
                          y_ref, route_ref, route_t_ref, count_ref, run_ref):
    @pl.when(pl.program_id(0) == 0)
    def _():
        run_ref[...] = jnp.zeros_like(run_ref)

    y = _layer_norm(DEEPNORM_ALPHA * x_ref[...] + _nn(o_ref[...], wo_ref[...]), g_ref[...], b_ref[...])
    y_ref[...] = y
    yh = y.astype(BF16)
    yl = (y - yh.astype(F32)).astype(BF16)
    logits = _nn(yh, rh_ref[...]) + (_nn(yl, rh_ref[...]) + _nn(yh, rl_ref[...])) + rb_ref[...]
    lane = lax.broadcasted_iota(jnp.int32, logits.shape, 1)
    is_g = (lane >= N_EXPERTS) & (lane < N_EXPERTS + N_GROUPS)
    gl = jnp.where(is_g, logits, NEG_INF)
    gmax = jnp.max(gl, axis=1, keepdims=True)
    gsel = jnp.min(jnp.where(gl == gmax, lane, 2 * LANES), axis=1, keepdims=True) - N_EXPERTS
    gprob = 1.0 / jnp.sum(jnp.exp(gl - gmax), axis=1, keepdims=True)
    in_g = (lane >= gsel * EXPERTS_PER_GROUP) & (lane < (gsel + 1) * EXPERTS_PER_GROUP)
    el = jnp.where(in_g, logits, NEG_INF)
    v1 = jnp.max(el, axis=1, keepdims=True)
    i1 = jnp.min(jnp.where(el == v1, lane, 2 * LANES), axis=1, keepdims=True)
    el2 = jnp.where(lane == i1, NEG_INF, el)
    v2 = jnp.max(el2, axis=1, keepdims=True)
    i2 = jnp.min(jnp.where(el2 == v2, lane, 2 * LANES), axis=1, keepdims=True)
    e2 = jnp.exp(v2 - v1)
    den = 1.0 + e2
    g1 = (1.0 / den) * gprob
    g2 = (e2 / den) * gprob
    hit1 = lane == i1
    hit2 = lane == i2
    oh = jnp.where(hit1 | hit2, 1.0, 0.0)
    pos = run_ref[...] + _nn(tri_ref[...], oh.astype(BF16))
    r1 = jnp.sum(jnp.where(hit1, pos, 0.0), axis=1, keepdims=True)
    r2 = jnp.sum(jnp.where(hit2, pos, 0.0), axis=1, keepdims=True)
    run_ref[...] = run_ref[...] + jnp.sum(oh, axis=0, keepdims=True)
    count_ref[...] = run_ref[...]
    rec = jnp.zeros(logits.shape, F32)
    for col, val in ((ROUTE_E1, i1.astype(F32)), (ROUTE_E2, i2.astype(F32)), (ROUTE_G1, g1), (ROUTE_G2, g2),
                     (ROUTE_R1, r1), (ROUTE_R2, r2)):
        rec = jnp.where(lane == col, val, rec)
    route_ref[...] = rec
    route_t_ref[...] = rec.T[0:8, :]


def _out_ln_router(o, x, wo, g, b, rh, rl, rb, tile):
    n = x.shape[0]
    row = lambda i: (i, 0)
    fix = lambda i: (0, 0)
    tri = jnp.tril(jnp.ones((tile, tile), BF16), -1)
    return pl.pallas_call(
        _out_ln_router_kernel,
        grid=(n // tile,),
        in_specs=[pl.BlockSpec((tile, D_MODEL), row), pl.BlockSpec((tile, D_MODEL), row),
                  pl.BlockSpec((D_MODEL, D_MODEL), fix), pl.BlockSpec((1, D_MODEL), fix),
                  pl.BlockSpec((1, D_MODEL), fix), pl.BlockSpec((D_MODEL, LANES), fix),
                  pl.BlockSpec((D_MODEL, LANES), fix), pl.BlockSpec((1, LANES), fix),
                  pl.BlockSpec((tile, tile), fix)],
        out_specs=[pl.BlockSpec((tile, D_MODEL), row), pl.BlockSpec((tile, LANES), row),
                   pl.BlockSpec((8, tile), lambda i: (0, i)), pl.BlockSpec((1, LANES), fix)],
        out_shape=[jax.ShapeDtypeStruct((n, D_MODEL), F32), jax.ShapeDtypeStruct((n, LANES), F32),
                   jax.ShapeDtypeStruct((8, n), F32), jax.ShapeDtypeStruct((1, LANES), F32)],
        scratch_shapes=[pltpu.VMEM((1, LANES), F32)],
        compiler_params=_cparams(("arbitrary",)),
        name="out_ln_router",
    )(o, x, wo, g, b, rh, rl, rb, tri)


MOE_ROW_TILE = 256


def _row_copy(src_ref, src_row, dst_ref, dst_row, sem):
    return pltpu.make_async_copy(src_ref.at[pl.ds(src_row, 1), :], dst_ref.at[pl.ds(dst_row, 1), :], sem)


def _moe_scatter_kernel(zt_ref, nu_ref, off_ref, ids_ref, y_ref, xs_ref, zbuf, sem, zsem, *, tm, n_tiles):
    t = y_ref.shape[0]

    @pl.when(pl.program_id(0) == 0)
    def _():
        zbuf[...] = jnp.zeros_like(zbuf)

        def fill(row0):
            return pltpu.make_async_copy(zbuf, xs_ref.at[pl.ds(pl.multiple_of(row0, tm), tm), :], zsem)

        for wait in (False, True):
            for e in range(N_EXPERTS):
                @pl.when(zt_ref[e] >= 0)
                def _(e=e, wait=wait):
                    fill(zt_ref[e]).wait() if wait else fill(zt_ref[e]).start()

            def tail(k, c, wait=wait):
                fill(k * tm).wait() if wait else fill(k * tm).start()
                return c

            lax.fori_loop(nu_ref[0], n_tiles, tail, 0)

    def start_row(r, c):
        for k in range(2):
            dst = off_ref[ids_ref[0, 0, k * t + r]] + ids_ref[0, 0, (2 + k) * t + r]
            _row_copy(y_ref, r, xs_ref, dst, sem).start(priority=k)
        return c

    def wait_row(r, c):
        for k in range(2):
            _row_copy(y_ref, 0, xs_ref, 0, sem).wait()
        return c

    lax.fori_loop(0, t, start_row, 0, unroll=8)
    lax.fori_loop(0, t, wait_row, 0, unroll=8)


def _moe_scatter(y, ids, ztile, n_used, offs, n_tiles, tile):
    n = y.shape[0]
    tm = MOE_ROW_TILE
    return pl.pallas_call(
        functools.partial(_moe_scatter_kernel, tm=tm, n_tiles=n_tiles),
        grid_spec=pltpu.PrefetchScalarGridSpec(
            num_scalar_prefetch=3, grid=(n // tile,),
            in_specs=[pl.BlockSpec((1, 1, 4 * tile), lambda i, zt, nu, off: (i, 0, 0), memory_space=pltpu.SMEM),
                      pl.BlockSpec((tile, D_MODEL), lambda i, zt, nu, off: (i, 0))],
            out_specs=pl.BlockSpec(memory_space=pl.ANY),
            scratch_shapes=[pltpu.VMEM((tm, D_MODEL), F32), pltpu.SemaphoreType.DMA(()),
                            pltpu.SemaphoreType.DMA(())]),
        out_shape=jax.ShapeDtypeStruct((n_tiles * tm, D_MODEL), F32),
        compiler_params=_cparams(("arbitrary",)),
        name="moe_scatter",
    )(ztile, n_used, offs, ids, y)


def _moe_experts_kernel(te_ref, nu_ref, x_ref, wgu_ref, wdn_ref, o_ref):
    i = pl.program_id(0)

    @pl.when(i < nu_ref[0])
    def _():
        gu = _nn(x_ref[...].astype(BF16), wgu_ref[0].astype(BF16))
        gate, up = gu[:, :D_FF_EXPERT], gu[:, D_FF_EXPERT:]
        h = (gate / (1.0 + jnp.exp(-gate))) * up
        o_ref[...] = _nn(h.astype(BF16), wdn_ref[0].astype(BF16))

    @pl.when(i >= nu_ref[0])
    def _():
        o_ref[...] = jnp.zeros_like(o_ref)


def _moe_experts(xs, wgu, wdn, tile_expert, n_used):
    tm = MOE_ROW_TILE
    n_tiles = xs.shape[0] // tm
    used = lambda i, nu: jnp.minimum(i, nu[0] - 1)
    return pl.pallas_call(
        _moe_experts_kernel,
        grid_spec=pltpu.PrefetchScalarGridSpec(
            num_scalar_prefetch=2, grid=(n_tiles,),
            in_specs=[pl.BlockSpec((tm, D_MODEL), lambda i, te, nu: (used(i, nu), 0)),
                      pl.BlockSpec((1, D_MODEL, 2 * D_FF_EXPERT), lambda i, te, nu: (te[used(i, nu)], 0, 0)),
                      pl.BlockSpec((1, D_FF_EXPERT, D_MODEL), lambda i, te, nu: (te[used(i, nu)], 0, 0))],
            out_specs=pl.BlockSpec((tm, D_MODEL), lambda i, te, nu: (i, 0))),
        out_shape=jax.ShapeDtypeStruct(xs.shape, F32),
        compiler_params=_cparams(("arbitrary",)),
        name="moe_experts",
    )(tile_expert, n_used, xs, wgu, wdn)


def _moe_combine_kernel(off_ref, dcur_ref, dnext_ref, x_ref, route_ref, g_ref, b_ref, ys_ref, o_ref, buf, sem):
    i = pl.program_id(0)
    t = x_ref.shape[0]
    slot = i % 2

    def gather(dref, slot_, wait):
        def body(r, c):
            for k in range(2):
                src = 0 if wait else off_ref[dref[0, 0, k * t + r]] + dref[0, 0, (2 + k) * t + r]
                cp = pltpu.make_async_copy(ys_ref.at[pl.ds(src, 1), :],
                                           buf.at[slot_, k, pl.ds(r, 1), :], sem.at[slot_])
                cp.wait() if wait else cp.start(priority=k)
            return c
        lax.fori_loop(0, t, body, 0, unroll=8)

    @pl.when(i == 0)
    def _():
        gather(dcur_ref, 0, False)

    @pl.when(i + 1 < pl.num_programs(0))
    def _():
        gather(dnext_ref, 1 - slot, False)

    gather(dcur_ref, slot, True)
    route = route_ref[...]
    lane = lax.broadcasted_iota(jnp.int32, route.shape, 1)
    g1 = jnp.sum(jnp.where(lane == ROUTE_G1, route, 0.0), axis=1, keepdims=True)
    g2 = jnp.sum(jnp.where(lane == ROUTE_G2, route, 0.0), axis=1, keepdims=True)
    y = g1 * buf[slot, 0] + g2 * buf[slot, 1]
    o_ref[...] = _layer_norm(DEEPNORM_ALPHA * x_ref[...] + y, g_ref[...], b_ref[...])


def _moe_combine(x, route, ids, offs, ys, g, b, tile):
    n = x.shape[0]
    nt = n // tile
    row = lambda i, off: (i, 0)
    fix = lambda i, off: (0, 0)
    return pl.pallas_call(
        _moe_combine_kernel,
        grid_spec=pltpu.PrefetchScalarGridSpec(
            num_scalar_prefetch=1, grid=(nt,),
            in_specs=[pl.BlockSpec((1, 1, 4 * tile), lambda i, off: (i, 0, 0), memory_space=pltpu.SMEM),
                      pl.BlockSpec((1, 1, 4 * tile), lambda i, off: (jnp.minimum(i + 1, nt - 1), 0, 0),
                                   memory_space=pltpu.SMEM),
                      pl.BlockSpec((tile, D_MODEL), row), pl.BlockSpec((tile, LANES), row),
                      pl.BlockSpec((1, D_MODEL), fix), pl.BlockSpec((1, D_MODEL), fix),
                      pl.BlockSpec(memory_space=pl.ANY)],
            out_specs=pl.BlockSpec((tile, D_MODEL), row),
            scratch_shapes=[pltpu.VMEM((2, 2, tile, D_MODEL), F32), pltpu.SemaphoreType.DMA((2,))]),
        out_shape=jax.ShapeDtypeStruct((n, D_MODEL), F32),
        compiler_params=_cparams(("arbitrary",)),
        name="moe_combine",
    )(offs, ids, ids, x, route, g, b, ys)


def _tile_ids(route_t, tile):
    rows = jnp.stack([route_t[f] for f in (ROUTE_E1, ROUTE_E2, ROUTE_R1, ROUTE_R2)]).astype(jnp.int32)
    return rows.reshape(4, -1, tile).transpose(1, 0, 2).reshape(-1, 1, 4 * tile)


def _sparse_moe(y, route, route_t, counts, wgu, wdn, g, b, tile, combine_tile):
    n = y.shape[0]
    tm = MOE_ROW_TILE
    n_tiles = -(-(2 * n + N_EXPERTS * (tm - 1)) // tm)
    cnt = counts[0, :N_EXPERTS].astype(jnp.int32)
    padded = ((cnt + tm - 1) // tm) * tm
    ends = jnp.cumsum(padded)
    offs = ends - padded
    n_used = (ends[-1] // tm).reshape(1)
    tile_row0 = jnp.arange(n_tiles, dtype=jnp.int32) * tm
    tile_expert = jnp.minimum(jnp.sum(tile_row0[:, None] >= ends[None, :], axis=1), N_EXPERTS - 1).astype(jnp.int32)
    ztile = jnp.where(padded > 0, ends - tm, -1).astype(jnp.int32)
    xs = _moe_scatter(y, _tile_ids(route_t, tile), ztile, n_used, offs, n_tiles, tile)
    ys = _moe_experts(xs, wgu, wdn, tile_expert, n_used)
    return _moe_combine(y, route, _tile_ids(route_t, combine_tile), offs, ys, g, b, combine_tile)


def _reduce0(x, op):
    r, w = x.shape
    g = next(g for g in (64, 32, 16, 8) if r % g == 0)
    return op(op(x.reshape(r // g, g, w), axis=0), axis=0, keepdims=True)


def _count(mask, axis):
    ones = jnp.where(mask, 1.0, 0.0)
    if axis == 0:
        return _reduce0(ones, jnp.sum)
    return jnp.sum(ones, axis=axis, keepdims=True)


def _order_key(x):
    bits = lax.bitcast_convert_type(x, jnp.int32)
    return jnp.where(bits < 0, bits ^ 0x7FFFFFFF, bits)


def _topk_mask(load_key, idx, k, axis, idx_bits):
    kf = float(k)
    r = jnp.where(_count(load_key() >= 0, axis) >= kf, 0, INT_MIN).astype(jnp.int32)

    def value_bit(i, r):
        cand = r | jnp.left_shift(jnp.int32(1), 30 - i)
        return jnp.where(_count(load_key() >= cand, axis) >= kf, cand, r)

    r = lax.fori_loop(0, 31, value_bit, r)
    need = kf - _count(load_key() > r, axis)
    n_tie = _count(load_key() == r, axis)

    def index_bit(i, j):
        cand = j | jnp.left_shift(jnp.int32(1), idx_bits - 1 - i)
        return jnp.where(_count((load_key() == r) & (idx < cand), axis) < need, cand, j)

    some_left_out = jnp.max(jnp.where(need < n_tie, 1.0, 0.0)) > 0.0
    j = lax.cond(some_left_out,
                 lambda: lax.fori_loop(0, idx_bits, index_bit, jnp.zeros_like(r)),
                 lambda: jnp.full_like(r, (1 << idx_bits) - 1))
    key = load_key()
    return (key > r) | ((key == r) & (idx <= j))


def _dsa_prompt_kernel(q_ref, kb_ref, vb_ref, kid_ref, qi_ref, wit_ref, o_ref, key_ref,
                       *, classes, topk, slot_kv):
    n = pl.program_id(1)
    for n0, cnt in classes:
        @pl.when((n >= n0) & (n < n0 + cnt))
        def _(n0=n0, cnt=cnt):
            _dsa_prompt_block(q_ref, kb_ref, vb_ref, kid_ref, qi_ref, wit_ref, o_ref, key_ref,
                              sk=(n0 + cnt) * BLOCK, topk=topk, slot_kv=slot_kv)


def _dsa_prompt_block(q_ref, kb_ref, vb_ref, kid_ref, qi_ref, wit_ref, o_ref, key_ref, *, sk, topk, slot_kv):
    t0 = pl.program_id(1) * BLOCK
    lane = lax.broadcasted_iota(jnp.int32, (BLOCK, LANES), 1)
    low = lane < HEAD_DIM
    s_idx = lax.broadcasted_iota(jnp.int32, (sk, BLOCK), 0)
    causal = s_idx <= t0 + lax.broadcasted_iota(jnp.int32, (sk, BLOCK), 1)

    kid = kid_ref[0:sk, :]
    sc = jnp.zeros((sk, BLOCK), F32)
    for hp in range(IDX_HEADS // 2):
        slab = qi_ref[:, hp * LANES:(hp + 1) * LANES]
        z = jnp.zeros_like(slab)
        qh2 = jnp.concatenate([jnp.where(low, slab, z), jnp.where(low, z, slab)], axis=0)
        d = jnp.maximum(_nt(kid, qh2), 0.0)
        sc = sc + d[:, :BLOCK] * wit_ref[2 * hp:2 * hp + 1, :] + d[:, BLOCK:] * wit_ref[2 * hp + 1:2 * hp + 2, :]
    key_ref[0:sk, :] = _order_key(jnp.where(causal, sc, NEG_INF))
    sel = _topk_mask(lambda: key_ref[0:sk, :], s_idx, topk, 0, (sk - 1).bit_length()) & causal
    bias = jnp.where(sel, 0.0, NEG_INF).T

    kb = kb_ref[0:sk, :]
    vb = vb_ref[0:sk, :]
    res = [None] * N_HEADS
    for kv in range(KV_HEADS_B):
        slots = [p for p in range(N_HEADS) if slot_kv[p] == kv]
        qes = []
        for p in slots:
            slab = q_ref[:, (p // 2) * LANES:(p // 2 + 1) * LANES]
            z = jnp.zeros_like(slab)
            qm = jnp.where(low, slab, z) if p % 2 == 0 else jnp.where(low, z, slab)
            qes.append(jnp.concatenate([qm, z] if kv // 2 == 0 else [z, qm], axis=1))
        s_all = _nt(jnp.concatenate(qes, axis=0), kb)
        prs, dens = [], []
        for i in range(len(slots)):
            s = s_all[i * BLOCK:(i + 1) * BLOCK, :] + bias
            pr = jnp.exp2(s - jnp.max(s, axis=1, keepdims=True))
            dens.append(jnp.sum(pr, axis=1, keepdims=True))
            prs.append(pr.astype(BF16))
        o_all = _nn(jnp.concatenate(prs, axis=0), vb)
        for i, p in enumerate(slots):
            res[p] = o_all[i * BLOCK:(i + 1) * BLOCK, (kv // 2) * LANES:(kv // 2 + 1) * LANES] / dens[i]
    for pair in range(N_HEADS // 2):
        o_ref[:, pair * LANES:(pair + 1) * LANES] = jnp.where(low, res[2 * pair], res[2 * pair + 1]).astype(o_ref.dtype)


def _dsa_prompt(q, kb, vb, kid, qi, wit, batch, seq, topk, slot_kv, n_classes=2):
    nb = seq // BLOCK
    per = -(-nb // n_classes)
    classes = tuple((n0, min(per, nb - n0)) for n0 in range(0, nb, per))
    qmap = lambda b, n: (b * nb + n, 0)
    return pl.pallas_call(
        functools.partial(_dsa_prompt_kernel, classes=classes, topk=topk, slot_kv=slot_kv),
        grid=(batch, nb),
        in_specs=[pl.BlockSpec((BLOCK, D_MODEL), qmap),
                  pl.BlockSpec((seq, 2 * LANES), lambda b, n: (b, 0)),
                  pl.BlockSpec((seq, 2 * LANES), lambda b, n: (b, 0)),
                  pl.BlockSpec((seq, LANES), lambda b, n: (b, 0)),
                  pl.BlockSpec((BLOCK, IDX_HEADS * IDX_DIM), qmap),
                  pl.BlockSpec((IDX_HEADS, BLOCK), lambda b, n: (0, b * nb + n))],
        out_specs=pl.BlockSpec((BLOCK, D_MODEL), qmap),
        out_shape=jax.ShapeDtypeStruct((batch * seq, D_MODEL), BF16),
        scratch_shapes=[pltpu.VMEM((seq, BLOCK), jnp.int32)],
        compiler_params=_cparams(("parallel", "arbitrary")),
        name="dsa_prompt",
    )(q, kb, vb, kid, qi, wit)


def _page_copies(pt_ref, pool_ref, buf_ref, sem, step, slot, pairb, page0, n_pages):
    cps = []
    for bi in range(pairb):
        for i in range(n_pages):
            page = pt_ref[step * pairb + bi, page0 + i]
            cps.append(pltpu.make_async_copy(
                pool_ref.at[page], buf_ref.at[slot, bi, :, pl.ds(i * PAGE_SIZE, PAGE_SIZE)], sem.at[slot]))
    return cps


def _dsa_sample_scores_kernel(pt_ref, pool_ref, qi_ref, w_ref, kin_ref, o_ref, buf_ref, sem, *, pairb, n_pages, dec_seq):
    g = pl.program_id(0)
    ng = pl.num_programs(0)
    slot = g % 2

    @pl.when(g == 0)
    def _():
        for cp in _page_copies(pt_ref, pool_ref, buf_ref, sem, 0, 0, pairb, 0, n_pages):
            cp.start()

    @pl.when(g + 1 < ng)
    def _():
        for cp in _page_copies(pt_ref, pool_ref, buf_ref, sem, g + 1, 1 - slot, pairb, 0, n_pages):
            cp.start()

    for cp in _page_copies(pt_ref, pool_ref, buf_ref, sem, g, slot, pairb, 0, n_pages):
        cp.wait()

    past = n_pages * PAGE_SIZE
    lane = lax.broadcasted_iota(jnp.int32, (1, LANES), 1)
    for bi in range(pairb):
        qi = qi_ref[bi]
        w = w_ref[bi]
        r = jnp.maximum(_nn(qi, buf_ref[slot, bi].astype(BF16)), 0.0) * w
        rn = jnp.maximum(_nt(qi, kin_ref[bi]), 0.0) * w
        for t in range(dec_seq):
            row = bi * dec_seq + t
            o_ref[row:row + 1, 0:past] = jnp.sum(r[t * IDX_HEADS:(t + 1) * IDX_HEADS], axis=0, keepdims=True)
            new = jnp.sum(rn[t * IDX_HEADS:(t + 1) * IDX_HEADS], axis=0, keepdims=True)
            o_ref[row:row + 1, past:past + LANES] = jnp.where(lane <= t, new, NEG_INF)


def _dsa_sample_scores(page_table, pool_t, qix, wcol, kin, dec_seq, pairb=2):
    bd, n_pages = page_table.shape
    past = n_pages * PAGE_SIZE
    rows = dec_seq * IDX_HEADS
    blk = lambda s: pl.BlockSpec((pairb,) + s, lambda g, pt: (g, 0, 0))
    return pl.pallas_call(
        functools.partial(_dsa_sample_scores_kernel, pairb=pairb, n_pages=n_pages, dec_seq=dec_seq),
        grid_spec=pltpu.PrefetchScalarGridSpec(
            num_scalar_prefetch=1, grid=(bd // pairb,),
            in_specs=[pl.BlockSpec(memory_space=pl.ANY), blk((rows, IDX_DIM)), blk((rows, 1)), blk((LANES, IDX_DIM))],
            out_specs=pl.BlockSpec((pairb * dec_seq, past + LANES), lambda g, pt: (g, 0)),
            scratch_shapes=[pltpu.VMEM((2, pairb, IDX_DIM, past), F32), pltpu.SemaphoreType.DMA((2,))]),
        out_shape=jax.ShapeDtypeStruct((bd * dec_seq, past + LANES), F32),
        compiler_params=_cparams(("arbitrary",)),
        name="dsa_sample_scores",
    )(page_table, pool_t, qix, wcol, kin)


def _select_bias_kernel(sc_ref, o_ref, key_ref, *, topk):
    idx = lax.broadcasted_iota(jnp.int32, sc_ref.shape, 1)
    key_ref[...] = _order_key(sc_ref[...])
    sel = _topk_mask(lambda: key_ref[...], idx, topk, 1, (sc_ref.shape[1] - 1).bit_length())
    o_ref[...] = jnp.where(sel & (sc_ref[...] > NEG_INF), 0.0, NEG_INF)


def _select_bias(sc, topk, tile):
    r, c = sc.shape
    return pl.pallas_call(
        functools.partial(_select_bias_kernel, topk=topk),
        grid=(r // tile,),
        in_specs=[pl.BlockSpec((tile, c), lambda i: (i, 0))],
        out_specs=pl.BlockSpec((tile, c), lambda i: (i, 0)),
        out_shape=jax.ShapeDtypeStruct((r, c), F32),
        scratch_shapes=[pltpu.VMEM((tile, c), jnp.int32)],
        compiler_params=_cparams(("parallel",)),
        name="select_bias",
    )(sc)


def _dsa_sample_attn_kernel(pt_ref, kpool_ref, vpool_ref, qe_ref, bias_ref, biasn_ref, kn_ref, vn_ref, o_ref,
                            kbuf, vbuf, ksem, vsem, m_ref, l_ref, acc_ref, *, pairb, chunk_pages, dec_seq):
    g = pl.program_id(0)
    c = pl.program_id(1)
    nch = pl.num_programs(1)
    lin = g * nch + c
    total = pl.num_programs(0) * nch
    slot = lin % 2

    def copies(step_lin, slot_):
        gg = step_lin // nch
        cc = step_lin % nch
        return (_page_copies(pt_ref, kpool_ref, kbuf, ksem, gg, slot_, pairb, cc * chunk_pages, chunk_pages)
                + _page_copies(pt_ref, vpool_ref, vbuf, vsem, gg, slot_, pairb, cc * chunk_pages, chunk_pages))

    @pl.when(lin == 0)
    def _():
        for cp in copies(0, 0):
            cp.start()

    @pl.when(lin + 1 < total)
    def _():
        for cp in copies(lin + 1, 1 - slot):
            cp.start()

    for cp in copies(lin, slot):
        cp.wait()

    @pl.when(c == 0)
    def _():
        m_ref[...] = jnp.full_like(m_ref, -1e30)
        l_ref[...] = jnp.zeros_like(l_ref)
        acc_ref[...] = jnp.zeros_like(acc_ref)

    rows = qe_ref.shape[1]
    per_tok = rows // dec_seq

    def expand(b4):
        return jnp.concatenate(
            [jnp.broadcast_to(b4[t:t + 1, :], (per_tok, b4.shape[1])) for t in range(dec_seq)], axis=0)

    def update(bi, s, pv):
        m_old = m_ref[bi]
        m_new = jnp.maximum(m_old, jnp.max(s, axis=1, keepdims=True))
        alpha = jnp.exp2(m_old - m_new)
        pr = jnp.exp2(s - m_new)
        l_ref[bi] = alpha * l_ref[bi] + jnp.sum(pr, axis=1, keepdims=True)
        acc_ref[bi] = alpha * acc_ref[bi] + pv(pr.astype(BF16))
        m_ref[bi] = m_new

    for bi in range(pairb):
        qe = qe_ref[bi]
        kt = kbuf[slot, bi].astype(BF16)
        vt = vbuf[slot, bi].astype(BF16)
        s = _nn(qe, kt) + expand(bias_ref[bi * dec_seq:(bi + 1) * dec_seq, :])
        update(bi, s, lambda pr: _nt(pr, vt))

    @pl.when(c == nch - 1)
    def _():
        for bi in range(pairb):
            qe = qe_ref[bi]
            s = _nt(qe, kn_ref[bi]) + expand(biasn_ref[bi * dec_seq:(bi + 1) * dec_seq, :])
            update(bi, s, lambda pr: _nn(pr, vn_ref[bi]))
            o_ref[bi] = acc_ref[bi] / l_ref[bi]


def _dsa_sample_attn(page_table, kpool_t, vpool_t, qe, bias, knew, vnew, dec_seq, pairb=2, chunk_pages=16):
    bd, n_pages = page_table.shape
    nch = n_pages // chunk_pages
    chunk = chunk_pages * PAGE_SIZE
    rows = qe.shape[1]
    kvw = qe.shape[2]
    blk = lambda s: pl.BlockSpec((pairb,) + s, lambda g, c, pt: (g, 0, 0))
    return pl.pallas_call(
        functools.partial(_dsa_sample_attn_kernel, pairb=pairb, chunk_pages=chunk_pages, dec_seq=dec_seq),
        grid_spec=pltpu.PrefetchScalarGridSpec(
            num_scalar_prefetch=1, grid=(bd // pairb, nch),
            in_specs=[pl.BlockSpec(memory_space=pl.ANY), pl.BlockSpec(memory_space=pl.ANY),
                      blk((rows, kvw)),
                      pl.BlockSpec((pairb * dec_seq, chunk), lambda g, c, pt: (g, c)),
                      pl.BlockSpec((pairb * dec_seq, LANES), lambda g, c, pt: (g, n_pages)),
                      blk((LANES, kvw)), blk((LANES, kvw))],
            out_specs=blk((rows, kvw)),
            scratch_shapes=[pltpu.VMEM((2, pairb, kvw, chunk), F32), pltpu.VMEM((2, pairb, kvw, chunk), F32),
                            pltpu.SemaphoreType.DMA((2,)), pltpu.SemaphoreType.DMA((2,)),
                            pltpu.VMEM((pairb, rows, 1), F32), pltpu.VMEM((pairb, rows, 1), F32),
                            pltpu.VMEM((pairb, rows, kvw), F32)]),
        out_shape=jax.ShapeDtypeStruct((bd, rows, kvw), F32),
        compiler_params=_cparams(("arbitrary", "arbitrary")),
        name="dsa_sample_attn",
    )(page_table, kpool_t, vpool_t, qe, bias, bias, knew, vnew)


def _pick_tile(n, candidates):
    for t in candidates:
        if n % t == 0:
            return t
    raise ValueError(f"no token tile for {n}")


def _hi_lo(w):
    hi = w.astype(BF16)
    return hi, (w - hi.astype(F32)).astype(BF16)


def _pad_rows(x, rows):
    return jnp.pad(x, ((0, 0), (0, rows - x.shape[1]), (0, 0)))


def _post_attention(o, x, w_o_phys, l, moe, ln, tile, moe_tile):
    (w_rg, b_rg, w_re, b_re, w_gu, w_dn) = moe
    (ln1_g, ln1_b, ln2_g, ln2_b) = ln
    rw = jnp.zeros((D_MODEL, LANES), F32).at[:, :N_EXPERTS].set(w_re[l])
    rw = rw.at[:, N_EXPERTS:N_EXPERTS + N_GROUPS].set(w_rg[l])
    rb = jnp.zeros((1, LANES), F32).at[0, :N_EXPERTS].set(b_re[l]).at[0, N_EXPERTS:N_EXPERTS + N_GROUPS].set(b_rg[l])
    rh, rl = _hi_lo(rw)
    y, route, route_t, counts = _out_ln_router(o, x, w_o_phys.astype(BF16), ln1_g[l].reshape(1, -1),
                                               ln1_b[l].reshape(1, -1), rh, rl, rb, tile)
    return _sparse_moe(y, route, route_t, counts, w_gu[l], w_dn[l], ln2_g[l].reshape(1, -1),
                       ln2_b[l].reshape(1, -1), tile, moe_tile)


def kernel(x_prompt, x_sample, cache_win_k, cache_win_v, cache_k, cache_v, cache_idx_k, page_table,
           a_w_qkv, a_b_qkv, a_sinks, a_w_o, b_w_in, b_w_o,
           moe_w_rg, moe_b_rg, moe_w_re, moe_b_re, moe_w_gu, moe_w_dn,
           ln1_g, ln1_b, ln2_g, ln2_b):
    B, S, _ = x_prompt.shape
    Bd, T, _ = x_sample.shape
    n_p, n_s = B * S, Bd * T
    n_tot = n_p + n_s
    past = page_table.shape[1] * PAGE_SIZE
    tile = _pick_tile(n_tot, (512, 256, 128, 64, 32, 16, 8))
    moe_tile = _pick_tile(n_tot, (256, 128, 64, 32, 16, 8))
    qd = N_HEADS * HEAD_DIM
    moe = (moe_w_rg, moe_b_rg, moe_w_re, moe_b_re, moe_w_gu, moe_w_dn)
    ln = (ln1_g, ln1_b, ln2_g, ln2_b)

    x = jnp.concatenate([x_prompt.reshape(n_p, D_MODEL), x_sample.reshape(n_s, D_MODEL)], axis=0)
    pos = jnp.concatenate([jnp.tile(jnp.arange(S, dtype=jnp.int32), B),
                           jnp.tile(past + jnp.arange(T, dtype=jnp.int32), Bd)])
    cos, sin = _rope_tables(pos)
    scale = HEAD_DIM ** -0.5

    a = 0
    kva = KV_HEADS_A * HEAD_DIM
    order_a = _slot_order(KV_HEADS_A)
    perm_a = _col_perm(order_a)
    wq, wk, wv = a_w_qkv[a][:, :qd], a_w_qkv[a][:, qd:qd + kva], a_w_qkv[a][:, qd + kva:]
    bq, bk, bv = a_b_qkv[a][:qd], a_b_qkv[a][qd:qd + kva], a_b_qkv[a][qd + kva:]
    w_a = jnp.concatenate([wq[:, perm_a], wk, wv], axis=1).astype(BF16)
    b_a = jnp.concatenate([bq[perm_a], bk, bv]).reshape(1, -1)
    plan_a = ((0, 8, True, scale, (0,)), (8, 1, True, 1.0, (1, 2)), (9, 1, False, 1.0, (3, 4)))
    q0, k0, k0b, v0, v0b = _project(x, w_a, b_a, cos, sin, plan_a,
                                    [(qd, BF16), (kva, F32), (kva, BF16), (kva, F32), (kva, BF16)], tile)
    sinks_phys = a_sinks[a][np.asarray(order_a)]
    o_p = _swa_prompt(q0, k0b, v0b, sinks_phys, B, S)

    w_win = cache_win_k.shape[2]
    qx = q0[n_p:].reshape(Bd, T, N_HEADS // 2, LANES).transpose(0, 2, 1, 3).reshape(Bd, T * N_HEADS // 2, LANES)
    kwt = cache_win_k[a].transpose(0, 2, 3, 1).reshape(Bd, kva, w_win)
    vwt = cache_win_v[a].transpose(0, 2, 3, 1).reshape(Bd, kva, w_win)
    k0s = k0[n_p:].reshape(Bd, T, kva)
    v0s = v0[n_p:].reshape(Bd, T, kva)
    sinkmat = jnp.zeros((T * N_HEADS // 2, LANES), F32).at[:, :2].set(
        jnp.repeat(sinks_phys.reshape(N_HEADS // 2, 2), T, axis=0))
    ox = _swa_sample(qx, kwt, _pad_rows(k0s, LANES), vwt, _pad_rows(v0s, LANES), sinkmat, T,
                     nb=_pick_tile(Bd, (8, 4, 2, 1)))
    o_s = ox.reshape(Bd, N_HEADS // 2, T, LANES).transpose(0, 2, 1, 3).reshape(n_s, qd).astype(BF16)

    x = _post_attention(jnp.concatenate([o_p, o_s], axis=0), x, a_w_o[a][perm_a, :], 0, moe, ln, tile, moe_tile)

    wp = min(WINDOW, S)
    win_k_prompt = k0[:n_p].reshape(B, S, KV_HEADS_A, HEAD_DIM)[:, S - wp:][None]
    win_v_prompt = v0[:n_p].reshape(B, S, KV_HEADS_A, HEAD_DIM)[:, S - wp:][None]
    win_k_sample = jnp.concatenate([cache_win_k[a], k0s.reshape(Bd, T, KV_HEADS_A, HEAD_DIM)], axis=1)[:, -w_win:][None]
    win_v_sample = jnp.concatenate([cache_win_v[a], v0s.reshape(Bd, T, KV_HEADS_A, HEAD_DIM)], axis=1)[:, -w_win:][None]

    bl = 0
    kvb = KV_HEADS_B * HEAD_DIM
    qid = IDX_HEADS * IDX_DIM
    order_b = _slot_order(KV_HEADS_B)
    perm_b = _col_perm(order_b)
    slot_kv = tuple(h // (N_HEADS // KV_HEADS_B) for h in order_b)
    w_in = b_w_in[bl]
    c0 = 0
    wq = w_in[:, c0:c0 + qd]; c0 += qd
    wk = w_in[:, c0:c0 + kvb]; c0 += kvb
    wv = w_in[:, c0:c0 + kvb]; c0 += kvb
    wqi = w_in[:, c0:c0 + qid]; c0 += qid
    wki = w_in[:, c0:c0 + IDX_DIM]; c0 += IDX_DIM
    wwi = w_in[:, c0:c0 + IDX_HEADS]
    w_b = jnp.concatenate([wq[:, perm_b], wk, wv, wqi, wki, wki], axis=1).astype(BF16)
    b_b = jnp.zeros((1, w_b.shape[1]), F32)
    plan_b = ((0, 8, True, scale * math.log2(math.e), (0,)), (8, 2, True, 1.0, (1, 2)), (10, 2, False, 1.0, (3, 7)),
              (12, 4, True, IDX_DIM ** -0.5, (4,)), (16, 1, True, 1.0, (5, 6)))
    q1, k1, k1b, v1, qi1, ki1, kid1, v1b = _project(
        x, w_b, b_b, cos, sin, plan_b,
        [(qd, BF16), (kvb, F32), (kvb, BF16), (kvb, F32), (qid, BF16), (LANES, F32), (LANES, BF16), (kvb, BF16)],
        tile)
    wwi_t = jnp.zeros((2 * IDX_HEADS, D_MODEL), F32).at[:IDX_HEADS].set(wwi.T).astype(BF16)
    wit1 = _project_t(x, wwi_t, IDX_HEADS ** -0.5, F32, tile)

    o_p = _dsa_prompt(q1, k1b, v1b, kid1, qi1, wit1, B, S, min(TOPK_MAX, S // 4), slot_kv)

    n_pool = cache_k.shape[1]
    pool_ik_t = cache_idx_k[bl].transpose(0, 2, 1)
    qix = qi1[n_p:].reshape(Bd, T * IDX_HEADS, IDX_DIM)
    wcol = wit1[:IDX_HEADS, n_p:].T.reshape(Bd, T * IDX_HEADS, 1)
    kin = _pad_rows(kid1[n_p:, :IDX_DIM].reshape(Bd, T, IDX_DIM), LANES)
    sc = _dsa_sample_scores(page_table, pool_ik_t, qix, wcol, kin, T)
    bias = _select_bias(sc, min(TOPK_MAX, (past + T) // 4), _pick_tile(n_s, (128, 64, 32, 16, 8)))
    kpool_t = cache_k[bl].transpose(0, 2, 3, 1).reshape(n_pool, kvb, PAGE_SIZE)
    vpool_t = cache_v[bl].transpose(0, 2, 3, 1).reshape(n_pool, kvb, PAGE_SIZE)
    onehot = (np.asarray(slot_kv)[:, None] == np.arange(KV_HEADS_B)[None, :]).astype(np.float32)
    qe = (q1[n_p:].reshape(n_s, N_HEADS, 1, HEAD_DIM) * jnp.asarray(onehot, BF16)[None, :, :, None])
    qe = qe.reshape(Bd, T * N_HEADS, kvb)
    knew = _pad_rows(k1b[n_p:].reshape(Bd, T, kvb), LANES)
    vnew = _pad_rows(v1b[n_p:].reshape(Bd, T, kvb), LANES)
    ox = _dsa_sample_attn(page_table, kpool_t, vpool_t, qe, bias, knew, vnew, T,
                          chunk_pages=_pick_tile(page_table.shape[1], (16, 8, 4, 2, 1)))
    ox = ox.reshape(Bd, T, N_HEADS, KV_HEADS_B, HEAD_DIM)
    o_s = ox[:, :, np.arange(N_HEADS), np.asarray(slot_kv), :].reshape(n_s, qd).astype(BF16)

    x = _post_attention(jnp.concatenate([o_p, o_s], axis=0), x, b_w_o[bl][perm_b, :], 1, moe, ln, tile, moe_tile)

    k_prompt = k1[:n_p].reshape(B, S, KV_HEADS_B, HEAD_DIM)[None]
    v_prompt = v1[:n_p].reshape(B, S, KV_HEADS_B, HEAD_DIM)[None]
    idx_k_prompt = ki1[:n_p, :IDX_DIM].reshape(B, S, IDX_DIM)[None]
    k_sample = k1[n_p:].reshape(Bd, T, KV_HEADS_B, HEAD_DIM)[None]
    v_sample = v1[n_p:].reshape(Bd, T, KV_HEADS_B, HEAD_DIM)[None]
    idx_k_sample = ki1[n_p:, :IDX_DIM].reshape(Bd, T, IDX_DIM)[None]

    return (x[:n_p].reshape(B, S, D_MODEL), x[n_p:].reshape(Bd, T, D_MODEL),
            win_k_prompt, win_v_prompt, win_k_sample, win_v_sample,
            k_prompt, v_prompt, idx_k_prompt, k_sample, v_sample, idx_k_sample)
```

```python
import functools
import math

import jax
import jax.numpy as jnp
import numpy as np
from jax import lax
from jax.experimental import pallas as pl
from jax.experimental.pallas import tpu as pltpu

D_MODEL = 1024
HEAD_DIM = 64
N_HEADS = 16
KV_HEADS_A = 2
KV_HEADS_B = 4
WINDOW = 128
BLOCK = 128
IDX_HEADS = 8
IDX_DIM = 64
TOPK_MAX = 256
N_GROUPS = 4
EXPERTS_PER_GROUP = 8
N_EXPERTS = N_GROUPS * EXPERTS_PER_GROUP
D_FF_EXPERT = 256
ROPE_THETA = 10000.0
LN_EPS = 1e-5
DEPTH = 2
DEEPNORM_ALPHA = (2 * DEPTH) ** 0.25
PAST_LEN = 8192
PAGE_SIZE = 128

LANES = 128
NEG_INF = float("-inf")
INT_MIN = -(2 ** 31)
VMEM_LIMIT = 56 * 1024 * 1024

F32 = jnp.float32
BF16 = jnp.bfloat16


def _slot_order(kv_heads):
    g = N_HEADS // kv_heads
    even = [h for h in range(N_HEADS) if (h // g) % 2 == 0]
    odd = [h for h in range(N_HEADS) if (h // g) % 2 == 1]
    order = []
    for a, b in zip(even, odd):
        order += [a, b]
    return order


def _col_perm(order):
    return np.concatenate([np.arange(h * HEAD_DIM, (h + 1) * HEAD_DIM) for h in order])


def _rope_tables(pos):
    half = HEAD_DIM // 2
    inv = jnp.exp(jnp.arange(half, dtype=F32) * (-2.0 * math.log(ROPE_THETA) / HEAD_DIM))
    ang = pos.astype(F32)[:, None] * inv[None, :]
    c, s = jnp.cos(ang), jnp.sin(ang)
    return jnp.concatenate([c, c, c, c], axis=1), jnp.concatenate([-s, s, -s, s], axis=1)


def _rope_slab(x, cos, sin, lane):
    swapped = jnp.where((lane % HEAD_DIM) < HEAD_DIM // 2,
                        pltpu.roll(x, LANES - HEAD_DIM // 2, axis=1),
                        pltpu.roll(x, HEAD_DIM // 2, axis=1))
    return x * cos + swapped * sin


def _nt(a, b):
    return lax.dot_general(a, b, (((1,), (1,)), ((), ())), preferred_element_type=F32)


def _nn(a, b):
    return jnp.dot(a, b, preferred_element_type=F32)


def _cparams(sem):
    return pltpu.CompilerParams(dimension_semantics=sem, vmem_limit_bytes=VMEM_LIMIT)


def _two_part_specs(tiles_a, tile, width, nargs=1):
    return [pl.BlockSpec((tile, width), lambda i, *_: (jnp.minimum(i, tiles_a - 1), 0)),
            pl.BlockSpec((tile, width), lambda i, *_: (jnp.maximum(i - tiles_a, 0), 0))]


def _two_part_load(a_ref, b_ref, tiles_a):
    return jnp.where(pl.program_id(0) < tiles_a, a_ref[...], b_ref[...])


def _split_parts(parts, tile):
    if len(parts) == 1:
        return parts[0], parts[0], parts[0].shape[0] // tile, parts[0].shape[0]
    a, b = parts
    assert a.shape[0] % tile == 0 and b.shape[0] % tile == 0
    return a, b, a.shape[0] // tile, a.shape[0] + b.shape[0]


def _proj_kernel(xa_ref, xb_ref, w_ref, b_ref, cos_ref, sin_ref, *out_refs, plan, tiles_a):
    xb = _two_part_load(xa_ref, xb_ref, tiles_a).astype(BF16)
    cos = cos_ref[...]
    sin = sin_ref[...]
    lane = lax.broadcasted_iota(jnp.int32, (xa_ref.shape[0], LANES), 1)
    for (s0, ns, rope, scale, ois) in plan:
        acc = _nn(xb, w_ref[:, s0 * LANES:(s0 + ns) * LANES]) + b_ref[:, s0 * LANES:(s0 + ns) * LANES]
        for j in range(ns):
            slab = acc[:, j * LANES:(j + 1) * LANES]
            if rope:
                slab = _rope_slab(slab, cos, sin, lane)
            if scale != 1.0:
                slab = slab * scale
            for oi in ois:
                o_ref = out_refs[oi]
                o_ref[:, j * LANES:(j + 1) * LANES] = slab.astype(o_ref.dtype)


def _project(x_parts, w, b, cos, sin, pos_block, plan, outs, tile):
    xa, xb, tiles_a, n = _split_parts(x_parts, tile)
    d = xa.shape[1]
    f = w.shape[1]
    out_shape = [jax.ShapeDtypeStruct((n, wd), dt) for wd, dt in outs]
    out_specs = [pl.BlockSpec((tile, wd), lambda i: (i, 0)) for wd, _ in outs]
    return pl.pallas_call(
        functools.partial(_proj_kernel, plan=plan, tiles_a=tiles_a),
        grid=(n // tile,),
        in_specs=_two_part_specs(tiles_a, tile, d) + [
                  pl.BlockSpec((d, f), lambda i: (0, 0)),
                  pl.BlockSpec((1, f), lambda i: (0, 0)),
                  pl.BlockSpec((tile, LANES), lambda i: (pos_block(i), 0)),
                  pl.BlockSpec((tile, LANES), lambda i: (pos_block(i), 0))],
        out_specs=out_specs,
        out_shape=out_shape,
        compiler_params=_cparams(("parallel",)),
        name="proj",
    )(xa, xb, w, b, cos, sin)


def _projT_kernel(x_ref, wt_ref, o_ref, *, scale):
    acc = _nt(wt_ref[...], x_ref[...].astype(BF16))
    if scale != 1.0:
        acc = acc * scale
    o_ref[...] = acc.astype(o_ref.dtype)


def _project_t(x, wt, scale, dtype, tile):
    n, d = x.shape
    f = wt.shape[0]
    return pl.pallas_call(
        functools.partial(_projT_kernel, scale=scale),
        grid=(n // tile,),
        in_specs=[pl.BlockSpec((tile, d), lambda i: (i, 0)),
                  pl.BlockSpec((f, d), lambda i: (0, 0))],
        out_specs=pl.BlockSpec((f, tile), lambda i: (0, i)),
        out_shape=jax.ShapeDtypeStruct((f, n), dtype),
        compiler_params=_cparams(("parallel",)),
        name="proj_t",
    )(x, wt)


def _swa_prompt_kernel(sink_ref, q_ref, kp_ref, kc_ref, vp_ref, vc_ref, o_ref):
    n = pl.program_id(1)
    kband = jnp.concatenate([kp_ref[...], kc_ref[...]], axis=0)
    vband = jnp.concatenate([vp_ref[...], vc_ref[...]], axis=0)
    qi = lax.broadcasted_iota(jnp.int32, (BLOCK, 2 * BLOCK), 0)
    kj = lax.broadcasted_iota(jnp.int32, (BLOCK, 2 * BLOCK), 1)
    rel = qi - kj + BLOCK
    valid = (rel >= 0) & (rel < WINDOW) & ((n > 0) | (kj >= BLOCK))
    lane = lax.broadcasted_iota(jnp.int32, (BLOCK, LANES), 1)
    low = lane < HEAD_DIM
    for pair in range(N_HEADS // 2):
        slab = q_ref[:, pair * LANES:(pair + 1) * LANES]
        outs = []
        for half in range(2):
            p = 2 * pair + half
            qe = jnp.where(low if half == 0 else ~low, slab, jnp.zeros_like(slab))
            s = jnp.where(valid, _nt(qe, kband), NEG_INF)
            sink = sink_ref[p]
            m = jnp.maximum(jnp.max(s, axis=1, keepdims=True), sink)
            pr = jnp.exp(s - m)
            den = jnp.sum(pr, axis=1, keepdims=True) + jnp.exp(sink - m)
            outs.append(_nn(pr.astype(BF16), vband) / den)
        o_ref[:, pair * LANES:(pair + 1) * LANES] = jnp.where(low, outs[0], outs[1]).astype(o_ref.dtype)


def _swa_prompt(q, kb, vb, sinks, batch, seq):
    nb = seq // BLOCK
    cur = lambda b, n, s: (b * nb + n, 0)
    prev = lambda b, n, s: (b * nb + jnp.maximum(n - 1, 0), 0)
    return pl.pallas_call(
        _swa_prompt_kernel,
        grid_spec=pltpu.PrefetchScalarGridSpec(
            num_scalar_prefetch=1, grid=(batch, nb),
            in_specs=[pl.BlockSpec((BLOCK, D_MODEL), cur),
                      pl.BlockSpec((BLOCK, LANES), prev), pl.BlockSpec((BLOCK, LANES), cur),
                      pl.BlockSpec((BLOCK, LANES), prev), pl.BlockSpec((BLOCK, LANES), cur)],
            out_specs=pl.BlockSpec((BLOCK, D_MODEL), cur)),
        out_shape=jax.ShapeDtypeStruct((batch * seq, D_MODEL), BF16),
        compiler_params=_cparams(("parallel", "parallel")),
        name="swa_prompt",
    )(sinks, q, kb, kb, vb, vb)


def _swa_sample_kernel(q_ref, kw_ref, kn_ref, vw_ref, vn_ref, sink_ref, o_ref, *, nb, dec_seq):
    w = kw_ref.shape[2]
    rows = q_ref.shape[1]
    lane = lax.broadcasted_iota(jnp.int32, (LANES, w), 0)
    lowf = lane < HEAD_DIM
    lane_n = lax.broadcasted_iota(jnp.int32, (LANES, LANES), 1)
    lown = lane_n < HEAD_DIM
    tok = lax.broadcasted_iota(jnp.int32, (rows, w), 0) % dec_seq
    col = lax.broadcasted_iota(jnp.int32, (rows, w), 1)
    valid_w = col > tok + (w - WINDOW)
    tok_n = lax.broadcasted_iota(jnp.int32, (rows, LANES), 0) % dec_seq
    col_n = lax.broadcasted_iota(jnp.int32, (rows, LANES), 1)
    valid_n = (col_n <= tok_n) & (col_n < dec_seq)
    lane_o = lax.broadcasted_iota(jnp.int32, (rows, LANES), 1)
    for i in range(nb):
        x = q_ref[i]
        kw = kw_ref[i].astype(BF16)
        vw = vw_ref[i].astype(BF16)
        kn = kn_ref[i].astype(BF16)
        vn = vn_ref[i].astype(BF16)
        outs = []
        for half in range(2):
            fm = lowf if half == 0 else ~lowf
            nm = lown if half == 0 else ~lown
            sw = jnp.where(valid_w, _nn(x, jnp.where(fm, kw, jnp.zeros_like(kw))), NEG_INF)
            sn = jnp.where(valid_n, _nt(x, jnp.where(nm, kn, jnp.zeros_like(kn))), NEG_INF)
            sink = sink_ref[:, half:half + 1]
            m = jnp.maximum(jnp.maximum(jnp.max(sw, axis=1, keepdims=True),
                                        jnp.max(sn, axis=1, keepdims=True)), sink)
            pw = jnp.exp(sw - m)
            pn = jnp.exp(sn - m)
            den = jnp.sum(pw, axis=1, keepdims=True) + jnp.sum(pn, axis=1, keepdims=True) + jnp.exp(sink - m)
            acc = _nt(pw.astype(BF16), vw) + _nn(pn.astype(BF16), vn)
            outs.append(acc / den)
        o_ref[i] = jnp.where(lane_o < HEAD_DIM, outs[0], outs[1]).astype(o_ref.dtype)


def _swa_sample(qx, kwt, knew, vwt, vnew, sinkmat, dec_seq, nb=8):
    bd, rows, _ = qx.shape
    w = kwt.shape[2]
    blk = lambda s: pl.BlockSpec((nb,) + s, lambda i: (i, 0, 0))
    return pl.pallas_call(
        functools.partial(_swa_sample_kernel, nb=nb, dec_seq=dec_seq),
        grid=(bd // nb,),
        in_specs=[blk((rows, LANES)), blk((LANES, w)), blk((LANES, LANES)), blk((LANES, w)), blk((LANES, LANES)),
                  pl.BlockSpec((rows, LANES), lambda i: (0, 0))],
        out_specs=blk((rows, LANES)),
        out_shape=jax.ShapeDtypeStruct((bd, rows, LANES), F32),
        compiler_params=_cparams(("parallel",)),
        name="swa_sample",
    )(qx, kwt, knew, vwt, vnew, sinkmat)


def _layer_norm(y, g, b):
    mu = jnp.mean(y, axis=1, keepdims=True)
    d = y - mu
    var = jnp.mean(d * d, axis=1, keepdims=True)
    return d * lax.rsqrt(var + LN_EPS) * g + b


ROUTE_E1, ROUTE_E2, ROUTE_G1, ROUTE_G2, ROUTE_R1, ROUTE_R2 = range(6)


def _out_ln_router_kernel(oa_ref, ob_ref, xa_ref, xb_ref, wo_ref, g_ref, b_ref, rh_ref, rl_ref, rb_ref, tri_ref,
                          y_ref, route_ref, route_t_ref, count_ref, run_ref, *, o_tiles_a, x_tiles_a):
    @pl.when(pl.program_id(0) == 0)
    def _():
        run_ref[...] = jnp.zeros_like(run_ref)

    o = _two_part_load(oa_ref, ob_ref, o_tiles_a)
    x = _two_part_load(xa_ref, xb_ref, x_tiles_a)
    y = _layer_norm(DEEPNORM_ALPHA * x + _nn(o, wo_ref[...]), g_ref[...], b_ref[...])
    y_ref[...] = y
    yh = y.astype(BF16)
    yl = (y - yh.astype(F32)).astype(BF16)
    logits = _nn(yh, rh_ref[...]) + (_nn(yl, rh_ref[...]) + _nn(yh, rl_ref[...])) + rb_ref[...]
    lane = lax.broadcasted_iota(jnp.int32, logits.shape, 1)
    is_g = (lane >= N_EXPERTS) & (lane < N_EXPERTS + N_GROUPS)
    gl = jnp.where(is_g, logits, NEG_INF)
    gmax = jnp.max(gl, axis=1, keepdims=True)
    gsel = jnp.min(jnp.where(gl == gmax, lane, 2 * LANES), axis=1, keepdims=True) - N_EXPERTS
    gprob = 1.0 / jnp.sum(jnp.exp(gl - gmax), axis=1, keepdims=True)
    in_g = (lane >= gsel * EXPERTS_PER_GROUP) & (lane < (gsel + 1) * EXPERTS_PER_GROUP)
    el = jnp.where(in_g, logits, NEG_INF)
    v1 = jnp.max(el, axis=1, keepdims=True)
    i1 = jnp.min(jnp.where(el == v1, lane, 2 * LANES), axis=1, keepdims=True)
    el2 = jnp.where(lane == i1, NEG_INF, el)
    v2 = jnp.max(el2, axis=1, keepdims=True)
    i2 = jnp.min(jnp.where(el2 == v2, lane, 2 * LANES), axis=1, keepdims=True)
    e2 = jnp.exp(v2 - v1)
    den = 1.0 + e2
    g1 = (1.0 / den) * gprob
    g2 = (e2 / den) * gprob
    hit1 = lane == i1
    hit2 = lane == i2
    oh = jnp.where(hit1 | hit2, 1.0, 0.0)
    pos = run_ref[...] + _nn(tri_ref[...], oh.astype(BF16))
    r1 = jnp.sum(jnp.where(hit1, pos, 0.0), axis=1, keepdims=True)
    r2 = jnp.sum(jnp.where(hit2, pos, 0.0), axis=1, keepdims=True)
    run_ref[...] = run_ref[...] + jnp.sum(oh, axis=0, keepdims=True)
    count_ref[...] = run_ref[...]
    rec = jnp.zeros(logits.shape, F32)
    for col, val in ((ROUTE_E1, i1.astype(F32)), (ROUTE_E2, i2.astype(F32)), (ROUTE_G1, g1), (ROUTE_G2, g2),
                     (ROUTE_R1, r1), (ROUTE_R2, r2)):
        rec = jnp.where(lane == col, val, rec)
    route_ref[...] = rec
    route_t_ref[...] = rec.T[0:8, :]


def _out_ln_router(o_parts, x_parts, wo, g, b, rh, rl, rb, tile):
    oa, ob, o_tiles_a, n = _split_parts(o_parts, tile)
    xa, xb, x_tiles_a, _ = _split_parts(x_parts, tile)
    row = lambda i: (i, 0)
    fix = lambda i: (0, 0)
    tri = jnp.tril(jnp.ones((tile, tile), BF16), -1)
    return pl.pallas_call(
        functools.partial(_out_ln_router_kernel, o_tiles_a=o_tiles_a, x_tiles_a=x_tiles_a),
        grid=(n // tile,),
        in_specs=_two_part_specs(o_tiles_a, tile, D_MODEL) + _two_part_specs(x_tiles_a, tile, D_MODEL) + [
                  pl.BlockSpec((D_MODEL, D_MODEL), fix), pl.BlockSpec((1, D_MODEL), fix),
                  pl.BlockSpec((1, D_MODEL), fix), pl.BlockSpec((D_MODEL, LANES), fix),
                  pl.BlockSpec((D_MODEL, LANES), fix), pl.BlockSpec((1, LANES), fix),
                  pl.BlockSpec((tile, tile), fix)],
        out_specs=[pl.BlockSpec((tile, D_MODEL), row), pl.BlockSpec((tile, LANES), row),
                   pl.BlockSpec((8, tile), lambda i: (0, i)), pl.BlockSpec((1, LANES), fix)],
        out_shape=[jax.ShapeDtypeStruct((n, D_MODEL), F32), jax.ShapeDtypeStruct((n, LANES), F32),
                   jax.ShapeDtypeStruct((8, n), F32), jax.ShapeDtypeStruct((1, LANES), F32)],
        scratch_shapes=[pltpu.VMEM((1, LANES), F32)],
        compiler_params=_cparams(("arbitrary",)),
        name="out_ln_router",
    )(oa, ob, xa, xb, wo, g, b, rh, rl, rb, tri)


MOE_ROW_TILE = 512


def _row_copy(src_ref, src_row, dst_ref, dst_row, sem):
    return pltpu.make_async_copy(src_ref.at[pl.ds(src_row, 1), :], dst_ref.at[pl.ds(dst_row, 1), :], sem)


def _moe_scatter_kernel(zt_ref, nu_ref, off_ref, ids_ref, y_ref, xs_ref, zbuf, sem, zsem, *, tm, n_tiles):
    t = y_ref.shape[0]

    @pl.when(pl.program_id(0) == 0)
    def _():
        zbuf[...] = jnp.zeros_like(zbuf)

        def fill(row0):
            return pltpu.make_async_copy(zbuf, xs_ref.at[pl.ds(pl.multiple_of(row0, tm), tm), :], zsem)

        for wait in (False, True):
            for e in range(N_EXPERTS):
                @pl.when(zt_ref[e] >= 0)
                def _(e=e, wait=wait):
                    fill(zt_ref[e]).wait() if wait else fill(zt_ref[e]).start()

            def tail(k, c, wait=wait):
                fill(k * tm).wait() if wait else fill(k * tm).start()
                return c

            lax.fori_loop(nu_ref[0], n_tiles, tail, 0)

    def start_row(r, c):
        for k in range(2):
            dst = off_ref[ids_ref[0, 0, k * t + r]] + ids_ref[0, 0, (2 + k) * t + r]
            _row_copy(y_ref, r, xs_ref, dst, sem).start(priority=k)
        return c

    def wait_row(r, c):
        for k in range(2):
            _row_copy(y_ref, 0, xs_ref, 0, sem).wait()
        return c

    lax.fori_loop(0, t, start_row, 0, unroll=8)
    lax.fori_loop(0, t, wait_row, 0, unroll=8)


def _moe_scatter(y, ids, ztile, n_used, offs, n_tiles, tile):
    n = y.shape[0]
    tm = MOE_ROW_TILE
    return pl.pallas_call(
        functools.partial(_moe_scatter_kernel, tm=tm, n_tiles=n_tiles),
        grid_spec=pltpu.PrefetchScalarGridSpec(
            num_scalar_prefetch=3, grid=(n // tile,),
            in_specs=[pl.BlockSpec((1, 1, 4 * tile), lambda i, zt, nu, off: (i, 0, 0), memory_space=pltpu.SMEM),
                      pl.BlockSpec((tile, D_MODEL), lambda i, zt, nu, off: (i, 0))],
            out_specs=pl.BlockSpec(memory_space=pl.ANY),
            scratch_shapes=[pltpu.VMEM((tm, D_MODEL), F32), pltpu.SemaphoreType.DMA(()),
                            pltpu.SemaphoreType.DMA(())]),
        out_shape=jax.ShapeDtypeStruct((n_tiles * tm, D_MODEL), F32),
        compiler_params=_cparams(("arbitrary",)),
        name="moe_scatter",
    )(ztile, n_used, offs, ids, y)


def _moe_experts_kernel(te_ref, nu_ref, x_ref, wgu_ref, wdn_ref, o_ref):
    i = pl.program_id(0)

    @pl.when(i < nu_ref[0])
    def _():
        gu = _nn(x_ref[...].astype(BF16), wgu_ref[0, 0].astype(BF16))
        gate, up = gu[:, :D_FF_EXPERT], gu[:, D_FF_EXPERT:]
        h = (gate / (1.0 + jnp.exp(-gate))) * up
        o_ref[...] = _nn(h.astype(BF16), wdn_ref[0, 0].astype(BF16))

    @pl.when(i >= nu_ref[0])
    def _():
        o_ref[...] = jnp.zeros_like(o_ref)


def _moe_experts(xs, wgu, wdn, layer, tile_expert, n_used):
    tm = MOE_ROW_TILE
    n_tiles = xs.shape[0] // tm
    used = lambda i, nu: jnp.minimum(i, nu[0] - 1)
    return pl.pallas_call(
        _moe_experts_kernel,
        grid_spec=pltpu.PrefetchScalarGridSpec(
            num_scalar_prefetch=2, grid=(n_tiles,),
            in_specs=[pl.BlockSpec((tm, D_MODEL), lambda i, te, nu: (used(i, nu), 0)),
                      pl.BlockSpec((1, 1, D_MODEL, 2 * D_FF_EXPERT),
                                   lambda i, te, nu: (layer, te[used(i, nu)], 0, 0)),
                      pl.BlockSpec((1, 1, D_FF_EXPERT, D_MODEL),
                                   lambda i, te, nu: (layer, te[used(i, nu)], 0, 0))],
            out_specs=pl.BlockSpec((tm, D_MODEL), lambda i, te, nu: (i, 0))),
        out_shape=jax.ShapeDtypeStruct(xs.shape, F32),
        compiler_params=_cparams(("arbitrary",)),
        name="moe_experts",
    )(tile_expert, n_used, xs, wgu, wdn)


def _moe_combine_kernel(off_ref, dcur_ref, dnext_ref, x_ref, route_ref, g_ref, b_ref, ys_ref, *rest, tiles_a):
    *o_refs, buf, sem = rest
    i = pl.program_id(0)
    t = x_ref.shape[0]
    slot = i % 2

    def gather(dref, slot_, wait):
        def body(r, c):
            for k in range(2):
                src = 0 if wait else off_ref[dref[0, 0, k * t + r]] + dref[0, 0, (2 + k) * t + r]
                cp = pltpu.make_async_copy(ys_ref.at[pl.ds(src, 1), :],
                                           buf.at[slot_, k, pl.ds(r, 1), :], sem.at[slot_])
                cp.wait() if wait else cp.start(priority=k)
            return c
        lax.fori_loop(0, t, body, 0, unroll=8)

    @pl.when(i == 0)
    def _():
        gather(dcur_ref, 0, False)

    @pl.when(i + 1 < pl.num_programs(0))
    def _():
        gather(dnext_ref, 1 - slot, False)

    gather(dcur_ref, slot, True)
    route = route_ref[...]
    lane = lax.broadcasted_iota(jnp.int32, route.shape, 1)
    g1 = jnp.sum(jnp.where(lane == ROUTE_G1, route, 0.0), axis=1, keepdims=True)
    g2 = jnp.sum(jnp.where(lane == ROUTE_G2, route, 0.0), axis=1, keepdims=True)
    y = g1 * buf[slot, 0] + g2 * buf[slot, 1]
    out = _layer_norm(DEEPNORM_ALPHA * x_ref[...] + y, g_ref[...], b_ref[...])
    if len(o_refs) == 1:
        o_refs[0][...] = out
    else:
        @pl.when(i < tiles_a)
        def _():
            o_refs[0][...] = out

        @pl.when(i >= tiles_a)
        def _():
            o_refs[1][...] = out


def _moe_combine(x, route, ids, offs, ys, g, b, tile, split_at=None):
    n = x.shape[0]
    nt = n // tile
    row = lambda i, off: (i, 0)
    fix = lambda i, off: (0, 0)
    if split_at is None:
        tiles_a = nt
        out_specs = pl.BlockSpec((tile, D_MODEL), row)
        out_shape = jax.ShapeDtypeStruct((n, D_MODEL), F32)
    else:
        tiles_a = split_at // tile
        out_specs = [pl.BlockSpec((tile, D_MODEL), lambda i, off: (jnp.minimum(i, tiles_a - 1), 0)),
                     pl.BlockSpec((tile, D_MODEL), lambda i, off: (jnp.maximum(i - tiles_a, 0), 0))]
        out_shape = [jax.ShapeDtypeStruct((split_at, D_MODEL), F32),
                     jax.ShapeDtypeStruct((n - split_at, D_MODEL), F32)]
    return pl.pallas_call(
        functools.partial(_moe_combine_kernel, tiles_a=tiles_a),
        grid_spec=pltpu.PrefetchScalarGridSpec(
            num_scalar_prefetch=1, grid=(nt,),
            in_specs=[pl.BlockSpec((1, 1, 4 * tile), lambda i, off: (i, 0, 0), memory_space=pltpu.SMEM),
                      pl.BlockSpec((1, 1, 4 * tile), lambda i, off: (jnp.minimum(i + 1, nt - 1), 0, 0),
                                   memory_space=pltpu.SMEM),
                      pl.BlockSpec((tile, D_MODEL), row), pl.BlockSpec((tile, LANES), row),
                      pl.BlockSpec((1, D_MODEL), fix), pl.BlockSpec((1, D_MODEL), fix),
                      pl.BlockSpec(memory_space=pl.ANY)],
            out_specs=out_specs,
            scratch_shapes=[pltpu.VMEM((2, 2, tile, D_MODEL), F32), pltpu.SemaphoreType.DMA((2,))]),
        out_shape=out_shape,
        compiler_params=_cparams(("arbitrary",)),
        name="moe_combine",
    )(offs, ids, ids, x, route, g, b, ys)


def _tile_ids(route_t, tile):
    rows = jnp.stack([route_t[f] for f in (ROUTE_E1, ROUTE_E2, ROUTE_R1, ROUTE_R2)]).astype(jnp.int32)
    return rows.reshape(4, -1, tile).transpose(1, 0, 2).reshape(-1, 1, 4 * tile)


def _sparse_moe(y, route, route_t, counts, wgu, wdn, layer, g, b, tile, combine_tile, split_at=None):
    n = y.shape[0]
    tm = MOE_ROW_TILE
    n_tiles = -(-(2 * n + N_EXPERTS * (tm - 1)) // tm)
    cnt = counts[0, :N_EXPERTS].astype(jnp.int32)
    padded = ((cnt + tm - 1) // tm) * tm
    ends = jnp.cumsum(padded)
    offs = ends - padded
    n_used = (ends[-1] // tm).reshape(1)
    tile_row0 = jnp.arange(n_tiles, dtype=jnp.int32) * tm
    tile_expert = jnp.minimum(jnp.sum(tile_row0[:, None] >= ends[None, :], axis=1), N_EXPERTS - 1).astype(jnp.int32)
    ztile = jnp.where(padded > 0, ends - tm, -1).astype(jnp.int32)
    xs = _moe_scatter(y, _tile_ids(route_t, tile), ztile, n_used, offs, n_tiles, tile)
    ys = _moe_experts(xs, wgu, wdn, layer, tile_expert, n_used)
    return _moe_combine(y, route, _tile_ids(route_t, combine_tile), offs, ys, g, b, combine_tile, split_at)


def _reduce0(x, op):
    r, w = x.shape
    g = next(g for g in (64, 32, 16, 8) if r % g == 0)
    return op(op(x.reshape(r // g, g, w), axis=0), axis=0, keepdims=True)


def _count(mask, axis):
    ones = jnp.where(mask, 1.0, 0.0)
    if axis == 0:
        return _reduce0(ones, jnp.sum)
    return jnp.sum(ones, axis=axis, keepdims=True)


def _order_key(x):
    bits = lax.bitcast_convert_type(x, jnp.int32)
    return jnp.where(bits < 0, bits ^ 0x7FFFFFFF, bits)


def _topk_mask(load_key, idx, k, axis, idx_bits):
    kf = float(k)
    r = jnp.where(_count(load_key() >= 0, axis) >= kf, 0, INT_MIN).astype(jnp.int32)

    def value_bit(i, r):
        cand = r | jnp.left_shift(jnp.int32(1), 30 - i)
        return jnp.where(_count(load_key() >= cand, axis) >= kf, cand, r)

    r = lax.fori_loop(0, 31, value_bit, r)
    need = kf - _count(load_key() > r, axis)
    n_tie = _count(load_key() == r, axis)

    def index_bit(i, j):
        cand = j | jnp.left_shift(jnp.int32(1), idx_bits - 1 - i)
        return jnp.where(_count((load_key() == r) & (idx < cand), axis) < need, cand, j)

    some_left_out = jnp.max(jnp.where(need < n_tie, 1.0, 0.0)) > 0.0
    j = lax.cond(some_left_out,
                 lambda: lax.fori_loop(0, idx_bits, index_bit, jnp.zeros_like(r)),
                 lambda: jnp.full_like(r, (1 << idx_bits) - 1))
    key = load_key()
    return (key > r) | ((key == r) & (idx <= j))


def _dsa_prompt_kernel(q_ref, kb_ref, vb_ref, kid_ref, qi_ref, wit_ref, o_ref, key_ref,
                       *, classes, topk, slot_kv):
    n = pl.program_id(1)
    for n0, cnt in classes:
        @pl.when((n >= n0) & (n < n0 + cnt))
        def _(n0=n0, cnt=cnt):
            _dsa_prompt_block(q_ref, kb_ref, vb_ref, kid_ref, qi_ref, wit_ref, o_ref, key_ref,
                              sk=(n0 + cnt) * BLOCK, topk=topk, slot_kv=slot_kv)


def _dsa_prompt_block(q_ref, kb_ref, vb_ref, kid_ref, qi_ref, wit_ref, o_ref, key_ref, *, sk, topk, slot_kv):
    t0 = pl.program_id(1) * BLOCK
    lane = lax.broadcasted_iota(jnp.int32, (BLOCK, LANES), 1)
    low = lane < HEAD_DIM
    s_idx = lax.broadcasted_iota(jnp.int32, (sk, BLOCK), 0)
    causal = s_idx <= t0 + lax.broadcasted_iota(jnp.int32, (sk, BLOCK), 1)

    kid = kid_ref[0:sk, :]
    sc = jnp.zeros((sk, BLOCK), F32)
    for hp in range(IDX_HEADS // 2):
        slab = qi_ref[:, hp * LANES:(hp + 1) * LANES]
        z = jnp.zeros_like(slab)
        qh2 = jnp.concatenate([jnp.where(low, slab, z), jnp.where(low, z, slab)], axis=0)
        d = jnp.maximum(_nt(kid, qh2), 0.0)
        sc = sc + d[:, :BLOCK] * wit_ref[2 * hp:2 * hp + 1, :] + d[:, BLOCK:] * wit_ref[2 * hp + 1:2 * hp + 2, :]
    key_ref[0:sk, :] = _order_key(jnp.where(causal, sc, NEG_INF))
    sel = _topk_mask(lambda: key_ref[0:sk, :], s_idx, topk, 0, (sk - 1).bit_length()) & causal
    bias = jnp.where(sel, 0.0, NEG_INF).T

    kb = kb_ref[0:sk, :]
    vb = vb_ref[0:sk, :]
    res = [None] * N_HEADS
    for kv in range(KV_HEADS_B):
        slots = [p for p in range(N_HEADS) if slot_kv[p] == kv]
        qes = []
        for p in slots:
            slab = q_ref[:, (p // 2) * LANES:(p // 2 + 1) * LANES]
            z = jnp.zeros_like(slab)
            qm = jnp.where(low, slab, z) if p % 2 == 0 else jnp.where(low, z, slab)
            qes.append(jnp.concatenate([qm, z] if kv // 2 == 0 else [z, qm], axis=1))
        s_all = _nt(jnp.concatenate(qes, axis=0), kb)
        prs, dens = [], []
        for i in range(len(slots)):
            s = s_all[i * BLOCK:(i + 1) * BLOCK, :] + bias
            pr = jnp.exp2(s - jnp.max(s, axis=1, keepdims=True))
            dens.append(jnp.sum(pr, axis=1, keepdims=True))
            prs.append(pr.astype(BF16))
        o_all = _nn(jnp.concatenate(prs, axis=0), vb)
        for i, p in enumerate(slots):
            res[p] = o_all[i * BLOCK:(i + 1) * BLOCK, (kv // 2) * LANES:(kv // 2 + 1) * LANES] / dens[i]
    for pair in range(N_HEADS // 2):
        o_ref[:, pair * LANES:(pair + 1) * LANES] = jnp.where(low, res[2 * pair], res[2 * pair + 1]).astype(o_ref.dtype)


def _dsa_prompt(q, kb, vb, kid, qi, wit, batch, seq, topk, slot_kv, n_classes=4):
    nb = seq // BLOCK
    per = -(-nb // n_classes)
    classes = tuple((n0, min(per, nb - n0)) for n0 in range(0, nb, per))
    qmap = lambda b, n: (b * nb + n, 0)
    return pl.pallas_call(
        functools.partial(_dsa_prompt_kernel, classes=classes, topk=topk, slot_kv=slot_kv),
        grid=(batch, nb),
        in_specs=[pl.BlockSpec((BLOCK, D_MODEL), qmap),
                  pl.BlockSpec((seq, 2 * LANES), lambda b, n: (b, 0)),
                  pl.BlockSpec((seq, 2 * LANES), lambda b, n: (b, 0)),
                  pl.BlockSpec((seq, LANES), lambda b, n: (b, 0)),
                  pl.BlockSpec((BLOCK, IDX_HEADS * IDX_DIM), qmap),
                  pl.BlockSpec((IDX_HEADS, BLOCK), lambda b, n: (0, b * nb + n))],
        out_specs=pl.BlockSpec((BLOCK, D_MODEL), qmap),
        out_shape=jax.ShapeDtypeStruct((batch * seq, D_MODEL), BF16),
        scratch_shapes=[pltpu.VMEM((seq, BLOCK), jnp.int32)],
        compiler_params=_cparams(("parallel", "arbitrary")),
        name="dsa_prompt",
    )(q, kb, vb, kid, qi, wit)


def _page_copies(pt_ref, pool_ref, buf_ref, sem, step, slot, pairb, page0, n_pages):
    cps = []
    for bi in range(pairb):
        for i in range(n_pages):
            page = pt_ref[step * pairb + bi, page0 + i]
            cps.append(pltpu.make_async_copy(
                pool_ref.at[page], buf_ref.at[slot, bi, :, pl.ds(i * PAGE_SIZE, PAGE_SIZE)], sem.at[slot]))
    return cps


def _dsa_sample_scores_kernel(pt_ref, pool_ref, qi_ref, w_ref, kin_ref, o_ref, buf_ref, sem, *, pairb, n_pages, dec_seq):
    g = pl.program_id(0)
    ng = pl.num_programs(0)
    slot = g % 2

    @pl.when(g == 0)
    def _():
        for cp in _page_copies(pt_ref, pool_ref, buf_ref, sem, 0, 0, pairb, 0, n_pages):
            cp.start()

    @pl.when(g + 1 < ng)
    def _():
        for cp in _page_copies(pt_ref, pool_ref, buf_ref, sem, g + 1, 1 - slot, pairb, 0, n_pages):
            cp.start()

    for cp in _page_copies(pt_ref, pool_ref, buf_ref, sem, g, slot, pairb, 0, n_pages):
        cp.wait()

    past = n_pages * PAGE_SIZE
    lane = lax.broadcasted_iota(jnp.int32, (1, LANES), 1)
    for bi in range(pairb):
        qi = qi_ref[bi]
        w = w_ref[bi]
        r = jnp.maximum(_nn(qi, buf_ref[slot, bi].astype(BF16)), 0.0) * w
        rn = jnp.maximum(_nt(qi, kin_ref[bi]), 0.0) * w
        for t in range(dec_seq):
            row = bi * dec_seq + t
            o_ref[row:row + 1, 0:past] = jnp.sum(r[t * IDX_HEADS:(t + 1) * IDX_HEADS], axis=0, keepdims=True)
            new = jnp.sum(rn[t * IDX_HEADS:(t + 1) * IDX_HEADS], axis=0, keepdims=True)
            o_ref[row:row + 1, past:past + LANES] = jnp.where(lane <= t, new, NEG_INF)


def _dsa_sample_scores(page_table, pool_t, qix, wcol, kin, dec_seq, pairb=2):
    bd, n_pages = page_table.shape
    past = n_pages * PAGE_SIZE
    rows = dec_seq * IDX_HEADS
    blk = lambda s: pl.BlockSpec((pairb,) + s, lambda g, pt: (g, 0, 0))
    return pl.pallas_call(
        functools.partial(_dsa_sample_scores_kernel, pairb=pairb, n_pages=n_pages, dec_seq=dec_seq),
        grid_spec=pltpu.PrefetchScalarGridSpec(
            num_scalar_prefetch=1, grid=(bd // pairb,),
            in_specs=[pl.BlockSpec(memory_space=pl.ANY), blk((rows, IDX_DIM)), blk((rows, 1)), blk((LANES, IDX_DIM))],
            out_specs=pl.BlockSpec((pairb * dec_seq, past + LANES), lambda g, pt: (g, 0)),
            scratch_shapes=[pltpu.VMEM((2, pairb, IDX_DIM, past), F32), pltpu.SemaphoreType.DMA((2,))]),
        out_shape=jax.ShapeDtypeStruct((bd * dec_seq, past + LANES), F32),
        compiler_params=_cparams(("arbitrary",)),
        name="dsa_sample_scores",
    )(page_table, pool_t, qix, wcol, kin)


def _select_bias_kernel(sc_ref, o_ref, key_ref, *, topk):
    idx = lax.broadcasted_iota(jnp.int32, sc_ref.shape, 1)
    key_ref[...] = _order_key(sc_ref[...])
    sel = _topk_mask(lambda: key_ref[...], idx, topk, 1, (sc_ref.shape[1] - 1).bit_length())
    o_ref[...] = jnp.where(sel & (sc_ref[...] > NEG_INF), 0.0, NEG_INF)


def _select_bias(sc, topk, tile):
    r, c = sc.shape
    return pl.pallas_call(
        functools.partial(_select_bias_kernel, topk=topk),
        grid=(r // tile,),
        in_specs=[pl.BlockSpec((tile, c), lambda i: (i, 0))],
        out_specs=pl.BlockSpec((tile, c), lambda i: (i, 0)),
        out_shape=jax.ShapeDtypeStruct((r, c), F32),
        scratch_shapes=[pltpu.VMEM((tile, c), jnp.int32)],
        compiler_params=_cparams(("parallel",)),
        name="select_bias",
    )(sc)


def _dsa_sample_attn_kernel(pt_ref, kpool_ref, vpool_ref, qe_ref, bias_ref, biasn_ref, kn_ref, vn_ref, o_ref,
                            kbuf, vbuf, ksem, vsem, m_ref, l_ref, acc_ref, *, pairb, chunk_pages, dec_seq):
    g = pl.program_id(0)
    c = pl.program_id(1)
    nch = pl.num_programs(1)
    lin = g * nch + c
    total = pl.num_programs(0) * nch
    slot = lin % 2

    def copies(step_lin, slot_):
        gg = step_lin // nch
        cc = step_lin % nch
        return (_page_copies(pt_ref, kpool_ref, kbuf, ksem, gg, slot_, pairb, cc * chunk_pages, chunk_pages)
                + _page_copies(pt_ref, vpool_ref, vbuf, vsem, gg, slot_, pairb, cc * chunk_pages, chunk_pages))

    @pl.when(lin == 0)
    def _():
        for cp in copies(0, 0):
            cp.start()

    @pl.when(lin + 1 < total)
    def _():
        for cp in copies(lin + 1, 1 - slot):
            cp.start()

    for cp in copies(lin, slot):
        cp.wait()

    @pl.when(c == 0)
    def _():
        m_ref[...] = jnp.full_like(m_ref, -1e30)
        l_ref[...] = jnp.zeros_like(l_ref)
        acc_ref[...] = jnp.zeros_like(acc_ref)

    rows = qe_ref.shape[1]
    per_tok = rows // dec_seq

    def expand(b4):
        return jnp.concatenate(
            [jnp.broadcast_to(b4[t:t + 1, :], (per_tok, b4.shape[1])) for t in range(dec_seq)], axis=0)

    def update(bi, s, pv):
        m_old = m_ref[bi]
        m_new = jnp.maximum(m_old, jnp.max(s, axis=1, keepdims=True))
        alpha = jnp.exp2(m_old - m_new)
        pr = jnp.exp2(s - m_new)
        l_ref[bi] = alpha * l_ref[bi] + jnp.sum(pr, axis=1, keepdims=True)
        acc_ref[bi] = alpha * acc_ref[bi] + pv(pr.astype(BF16))
        m_ref[bi] = m_new

    for bi in range(pairb):
        qe = qe_ref[bi]
        kt = kbuf[slot, bi].astype(BF16)
        vt = vbuf[slot, bi].astype(BF16)
        s = _nn(qe, kt) + expand(bias_ref[bi * dec_seq:(bi + 1) * dec_seq, :])
        update(bi, s, lambda pr: _nt(pr, vt))

    @pl.when(c == nch - 1)
    def _():
        for bi in range(pairb):
            qe = qe_ref[bi]
            s = _nt(qe, kn_ref[bi]) + expand(biasn_ref[bi * dec_seq:(bi + 1) * dec_seq, :])
            update(bi, s, lambda pr: _nn(pr, vn_ref[bi]))
            o_ref[bi] = acc_ref[bi] / l_ref[bi]


def _dsa_sample_attn(page_table, kpool_t, vpool_t, qe, bias, knew, vnew, dec_seq, pairb=2, chunk_pages=16):
    bd, n_pages = page_table.shape
    nch = n_pages // chunk_pages
    chunk = chunk_pages * PAGE_SIZE
    rows = qe.shape[1]
    kvw = qe.shape[2]
    blk = lambda s: pl.BlockSpec((pairb,) + s, lambda g, c, pt: (g, 0, 0))
    return pl.pallas_call(
        functools.partial(_dsa_sample_attn_kernel, pairb=pairb, chunk_pages=chunk_pages, dec_seq=dec_seq),
        grid_spec=pltpu.PrefetchScalarGridSpec(
            num_scalar_prefetch=1, grid=(bd // pairb, nch),
            in_specs=[pl.BlockSpec(memory_space=pl.ANY), pl.BlockSpec(memory_space=pl.ANY),
                      blk((rows, kvw)),
                      pl.BlockSpec((pairb * dec_seq, chunk), lambda g, c, pt: (g, c)),
                      pl.BlockSpec((pairb * dec_seq, LANES), lambda g, c, pt: (g, n_pages)),
                      blk((LANES, kvw)), blk((LANES, kvw))],
            out_specs=blk((rows, kvw)),
            scratch_shapes=[pltpu.VMEM((2, pairb, kvw, chunk), F32), pltpu.VMEM((2, pairb, kvw, chunk), F32),
                            pltpu.SemaphoreType.DMA((2,)), pltpu.SemaphoreType.DMA((2,)),
                            pltpu.VMEM((pairb, rows, 1), F32), pltpu.VMEM((pairb, rows, 1), F32),
                            pltpu.VMEM((pairb, rows, kvw), F32)]),
        out_shape=jax.ShapeDtypeStruct((bd, rows, kvw), F32),
        compiler_params=_cparams(("arbitrary", "arbitrary")),
        name="dsa_sample_attn",
    )(page_table, kpool_t, vpool_t, qe, bias, bias, knew, vnew)


def _pick_tile(n, candidates):
    for t in candidates:
        if n % t == 0:
            return t
    raise ValueError(f"no token tile for {n}")


def _hi_lo(w):
    hi = w.astype(BF16)
    return hi, (w - hi.astype(F32)).astype(BF16)


def _pad_rows(x, rows):
    return jnp.pad(x, ((0, 0), (0, rows - x.shape[1]), (0, 0)))


def _post_attention(o_parts, x_parts, w_o_phys, l, moe, ln, tile, moe_tile, split_at=None):
    (w_rg, b_rg, w_re, b_re, w_gu, w_dn) = moe
    (ln1_g, ln1_b, ln2_g, ln2_b) = ln
    rw = jnp.zeros((D_MODEL, LANES), F32).at[:, :N_EXPERTS].set(w_re[l])
    rw = rw.at[:, N_EXPERTS:N_EXPERTS + N_GROUPS].set(w_rg[l])
    rb = jnp.zeros((1, LANES), F32).at[0, :N_EXPERTS].set(b_re[l]).at[0, N_EXPERTS:N_EXPERTS + N_GROUPS].set(b_rg[l])
    rh, rl = _hi_lo(rw)
    y, route, route_t, counts = _out_ln_router(o_parts, x_parts, w_o_phys.astype(BF16), ln1_g[l].reshape(1, -1),
                                               ln1_b[l].reshape(1, -1), rh, rl, rb, tile)
    return _sparse_moe(y, route, route_t, counts, w_gu, w_dn, l, ln2_g[l].reshape(1, -1),
                       ln2_b[l].reshape(1, -1), tile, moe_tile, split_at)


def kernel(x_prompt, x_sample, cache_win_k, cache_win_v, cache_k, cache_v, cache_idx_k, page_table,
           a_w_qkv, a_b_qkv, a_sinks, a_w_o, b_w_in, b_w_o,
           moe_w_rg, moe_b_rg, moe_w_re, moe_b_re, moe_w_gu, moe_w_dn,
           ln1_g, ln1_b, ln2_g, ln2_b):
    B, S, _ = x_prompt.shape
    Bd, T, _ = x_sample.shape
    n_p, n_s = B * S, Bd * T
    n_tot = n_p + n_s
    past = page_table.shape[1] * PAGE_SIZE
    tile = _pick_tile(math.gcd(S, n_s), (512, 256, 128))
    moe_tile = _pick_tile(tile, (256, 128))
    qd = N_HEADS * HEAD_DIM
    moe = (moe_w_rg, moe_b_rg, moe_w_re, moe_b_re, moe_w_gu, moe_w_dn)
    ln = (ln1_g, ln1_b, ln2_g, ln2_b)

    x_parts = (x_prompt.reshape(n_p, D_MODEL), x_sample.reshape(n_s, D_MODEL))
    pos = jnp.concatenate([jnp.arange(S, dtype=jnp.int32), past + jnp.arange(tile, dtype=jnp.int32) % T])
    cos, sin = _rope_tables(pos)
    pos_block = lambda i: jnp.where(i < n_p // tile, i % (S // tile), S // tile)
    scale = HEAD_DIM ** -0.5

    a = 0
    kva = KV_HEADS_A * HEAD_DIM
    order_a = _slot_order(KV_HEADS_A)
    perm_a = _col_perm(order_a)
    wq, wk, wv = a_w_qkv[a][:, :qd], a_w_qkv[a][:, qd:qd + kva], a_w_qkv[a][:, qd + kva:]
    bq, bk, bv = a_b_qkv[a][:qd], a_b_qkv[a][qd:qd + kva], a_b_qkv[a][qd + kva:]
    w_a = jnp.concatenate([wq[:, perm_a], wk, wv], axis=1).astype(BF16)
    b_a = jnp.concatenate([bq[perm_a], bk, bv]).reshape(1, -1)
    plan_a = ((0, 8, True, scale, (0,)), (8, 1, True, 1.0, (1, 2)), (9, 1, False, 1.0, (3, 4)))
    q0, k0, k0b, v0, v0b = _project(x_parts, w_a, b_a, cos, sin, pos_block, plan_a,
                                    [(qd, BF16), (kva, F32), (kva, BF16), (kva, F32), (kva, BF16)], tile)
    sinks_phys = a_sinks[a][np.asarray(order_a)]
    o_p = _swa_prompt(q0, k0b, v0b, sinks_phys, B, S)

    w_win = cache_win_k.shape[2]
    qx = q0[n_p:].reshape(Bd, T, N_HEADS // 2, LANES).transpose(0, 2, 1, 3).reshape(Bd, T * N_HEADS // 2, LANES)
    kwt = cache_win_k[a].transpose(0, 2, 3, 1).reshape(Bd, kva, w_win)
    vwt = cache_win_v[a].transpose(0, 2, 3, 1).reshape(Bd, kva, w_win)
    k0s = k0[n_p:].reshape(Bd, T, kva)
    v0s = v0[n_p:].reshape(Bd, T, kva)
    sinkmat = jnp.zeros((T * N_HEADS // 2, LANES), F32).at[:, :2].set(
        jnp.repeat(sinks_phys.reshape(N_HEADS // 2, 2), T, axis=0))
    ox = _swa_sample(qx, kwt, _pad_rows(k0s, LANES), vwt, _pad_rows(v0s, LANES), sinkmat, T,
                     nb=_pick_tile(Bd, (8, 4, 2, 1)))
    o_s = ox.reshape(Bd, N_HEADS // 2, T, LANES).transpose(0, 2, 1, 3).reshape(n_s, qd).astype(BF16)

    x = _post_attention((o_p, o_s), x_parts, a_w_o[a][perm_a, :], 0, moe, ln, tile, moe_tile)

    wp = min(WINDOW, S)
    win_k_prompt = k0[:n_p].reshape(B, S, KV_HEADS_A, HEAD_DIM)[:, S - wp:][None]
    win_v_prompt = v0[:n_p].reshape(B, S, KV_HEADS_A, HEAD_DIM)[:, S - wp:][None]
    win_k_sample = jnp.concatenate([cache_win_k[a], k0s.reshape(Bd, T, KV_HEADS_A, HEAD_DIM)], axis=1)[:, -w_win:][None]
    win_v_sample = jnp.concatenate([cache_win_v[a], v0s.reshape(Bd, T, KV_HEADS_A, HEAD_DIM)], axis=1)[:, -w_win:][None]

    bl = 0
    kvb = KV_HEADS_B * HEAD_DIM
    qid = IDX_HEADS * IDX_DIM
    order_b = _slot_order(KV_HEADS_B)
    perm_b = _col_perm(order_b)
    slot_kv = tuple(h // (N_HEADS // KV_HEADS_B) for h in order_b)
    w_in = b_w_in[bl]
    c0 = 0
    wq = w_in[:, c0:c0 + qd]; c0 += qd
    wk = w_in[:, c0:c0 + kvb]; c0 += kvb
    wv = w_in[:, c0:c0 + kvb]; c0 += kvb
    wqi = w_in[:, c0:c0 + qid]; c0 += qid
    wki = w_in[:, c0:c0 + IDX_DIM]; c0 += IDX_DIM
    wwi = w_in[:, c0:c0 + IDX_HEADS]
    w_b = jnp.concatenate([wq[:, perm_b], wk, wv, wqi, wki, wki], axis=1).astype(BF16)
    b_b = jnp.zeros((1, w_b.shape[1]), F32)
    plan_b = ((0, 8, True, scale * math.log2(math.e), (0,)), (8, 2, True, 1.0, (1, 2)), (10, 2, False, 1.0, (3, 7)),
              (12, 4, True, IDX_DIM ** -0.5, (4,)), (16, 1, True, 1.0, (5, 6)))
    q1, k1, k1b, v1, qi1, ki1, kid1, v1b = _project(
        (x,), w_b, b_b, cos, sin, pos_block, plan_b,
        [(qd, BF16), (kvb, F32), (kvb, BF16), (kvb, F32), (qid, BF16), (LANES, F32), (LANES, BF16), (kvb, BF16)],
        tile)
    wwi_t = jnp.zeros((2 * IDX_HEADS, D_MODEL), F32).at[:IDX_HEADS].set(wwi.T).astype(BF16)
    wit1 = _project_t(x, wwi_t, IDX_HEADS ** -0.5, F32, tile)

    o_p = _dsa_prompt(q1, k1b, v1b, kid1, qi1, wit1, B, S, min(TOPK_MAX, S // 4), slot_kv)

    n_pool = cache_k.shape[1]
    pool_ik_t = cache_idx_k[bl].transpose(0, 2, 1)
    qix = qi1[n_p:].reshape(Bd, T * IDX_HEADS, IDX_DIM)
    wcol = wit1[:IDX_HEADS, n_p:].T.reshape(Bd, T * IDX_HEADS, 1)
    kin = _pad_rows(kid1[n_p:, :IDX_DIM].reshape(Bd, T, IDX_DIM), LANES)
    sc = _dsa_sample_scores(page_table, pool_ik_t, qix, wcol, kin, T)
    bias = _select_bias(sc, min(TOPK_MAX, (past + T) // 4), _pick_tile(n_s, (128, 64, 32, 16, 8)))
    kpool_t = cache_k[bl].transpose(0, 2, 3, 1).reshape(n_pool, kvb, PAGE_SIZE)
    vpool_t = cache_v[bl].transpose(0, 2, 3, 1).reshape(n_pool, kvb, PAGE_SIZE)
    onehot = (np.asarray(slot_kv)[:, None] == np.arange(KV_HEADS_B)[None, :]).astype(np.float32)
    qe = (q1[n_p:].reshape(n_s, N_HEADS, 1, HEAD_DIM) * jnp.asarray(onehot, BF16)[None, :, :, None])
    qe = qe.reshape(Bd, T * N_HEADS, kvb)
    knew = _pad_rows(k1b[n_p:].reshape(Bd, T, kvb), LANES)
    vnew = _pad_rows(v1b[n_p:].reshape(Bd, T, kvb), LANES)
    ox = _dsa_sample_attn(page_table, kpool_t, vpool_t, qe, bias, knew, vnew, T,
                          chunk_pages=_pick_tile(page_table.shape[1], (16, 8, 4, 2, 1)))
    ox = ox.reshape(Bd, T, N_HEADS, KV_HEADS_B, HEAD_DIM)
    o_s = ox[:, :, np.arange(N_HEADS), np.asarray(slot_kv), :].reshape(n_s, qd).astype(BF16)

    y_p, y_s = _post_attention((o_p, o_s), (x,), b_w_o[bl][perm_b, :], 1, moe, ln, tile, moe_tile, split_at=n_p)

    k_prompt = k1[:n_p].reshape(B, S, KV_HEADS_B, HEAD_DIM)[None]
    v_prompt = v1[:n_p].reshape(B, S, KV_HEADS_B, HEAD_DIM)[None]
    idx_k_prompt = ki1[:n_p, :IDX_DIM].reshape(B, S, IDX_DIM)[None]
    k_sample = k1[n_p:].reshape(Bd, T, KV_HEADS_B, HEAD_DIM)[None]
    v_sample = v1[n_p:].reshape(Bd, T, KV_HEADS_B, HEAD_DIM)[None]
    idx_k_sample = ki1[n_p:, :IDX_DIM].reshape(Bd, T, IDX_DIM)[None]

    return (y_p.reshape(B, S, D_MODEL), y_s.reshape(Bd, T, D_MODEL),
            win_k_prompt, win_v_prompt, win_k_sample, win_v_sample,
            k_prompt, v_prompt, idx_k_prompt, k_sample, v_sample, idx_k_sample)
```

```python
import functools
import math

import jax
import jax.numpy as jnp
import numpy as np
from jax import lax
from jax.experimental import pallas as pl
from jax.experimental.pallas import tpu as pltpu

D_MODEL = 1024
HEAD_DIM = 64
N_HEADS = 16
KV_HEADS_A = 2
KV_HEADS_B = 4
WINDOW = 128
BLOCK = 128
IDX_HEADS = 8
IDX_DIM = 64
TOPK_MAX = 256
N_GROUPS = 4
EXPERTS_PER_GROUP = 8
N_EXPERTS = N_GROUPS * EXPERTS_PER_GROUP
D_FF_EXPERT = 256
ROPE_THETA = 10000.0
LN_EPS = 1e-5
DEPTH = 2
DEEPNORM_ALPHA = (2 * DEPTH) ** 0.25
PAST_LEN = 8192
PAGE_SIZE = 128

LANES = 128
NEG_INF = float("-inf")
INT_MIN = -(2 ** 31)
VMEM_LIMIT = 56 * 1024 * 1024

F32 = jnp.float32
BF16 = jnp.bfloat16


def _slot_order(kv_heads):
    g = N_HEADS // kv_heads
    even = [h for h in range(N_HEADS) if (h // g) % 2 == 0]
    odd = [h for h in range(N_HEADS) if (h // g) % 2 == 1]
    order = []
    for a, b in zip(even, odd):
        order += [a, b]
    return order


def _col_perm(order):
    return np.concatenate([np.arange(h * HEAD_DIM, (h + 1) * HEAD_DIM) for h in order])


def _rope_tables(pos):
    half = HEAD_DIM // 2
    inv = jnp.exp(jnp.arange(half, dtype=F32) * (-2.0 * math.log(ROPE_THETA) / HEAD_DIM))
    ang = pos.astype(F32)[:, None] * inv[None, :]
    c, s = jnp.cos(ang), jnp.sin(ang)
    return jnp.concatenate([c, c, c, c], axis=1), jnp.concatenate([-s, s, -s, s], axis=1)


def _rope_slab(x, cos, sin, lane):
    swapped = jnp.where((lane % HEAD_DIM) < HEAD_DIM // 2,
                        pltpu.roll(x, LANES - HEAD_DIM // 2, axis=1),
                        pltpu.roll(x, HEAD_DIM // 2, axis=1))
    return x * cos + swapped * sin


def _nt(a, b):
    return lax.dot_general(a, b, (((1,), (1,)), ((), ())), preferred_element_type=F32)


def _nn(a, b):
    return jnp.dot(a, b, preferred_element_type=F32)


def _cparams(sem):
    return pltpu.CompilerParams(dimension_semantics=sem, vmem_limit_bytes=VMEM_LIMIT)


def _two_part_specs(tiles_a, tile, width, nargs=1):
    return [pl.BlockSpec((tile, width), lambda i, *_: (jnp.minimum(i, tiles_a - 1), 0)),
            pl.BlockSpec((tile, width), lambda i, *_: (jnp.maximum(i - tiles_a, 0), 0))]


def _two_part_load(a_ref, b_ref, tiles_a):
    return jnp.where(pl.program_id(0) < tiles_a, a_ref[...], b_ref[...])


def _split_parts(parts, tile):
    if len(parts) == 1:
        return parts[0], parts[0], parts[0].shape[0] // tile, parts[0].shape[0]
    a, b = parts
    assert a.shape[0] % tile == 0 and b.shape[0] % tile == 0
    return a, b, a.shape[0] // tile, a.shape[0] + b.shape[0]


def _proj_kernel(xa_ref, xb_ref, w_ref, b_ref, cos_ref, sin_ref, *out_refs, plan, tiles_a):
    xb = _two_part_load(xa_ref, xb_ref, tiles_a).astype(BF16)
    cos = cos_ref[...]
    sin = sin_ref[...]
    lane = lax.broadcasted_iota(jnp.int32, (xa_ref.shape[0], LANES), 1)
    for (s0, ns, rope, scale, ois) in plan:
        acc = _nn(xb, w_ref[:, s0 * LANES:(s0 + ns) * LANES]) + b_ref[:, s0 * LANES:(s0 + ns) * LANES]
        for j in range(ns):
            slab = acc[:, j * LANES:(j + 1) * LANES]
            if rope:
                slab = _rope_slab(slab, cos, sin, lane)
            if scale != 1.0:
                slab = slab * scale
            for oi in ois:
                o_ref = out_refs[oi]
                o_ref[:, j * LANES:(j + 1) * LANES] = slab.astype(o_ref.dtype)


def _project(x_parts, w, b, cos, sin, pos_block, plan, outs, tile):
    xa, xb, tiles_a, n = _split_parts(x_parts, tile)
    d = xa.shape[1]
    f = w.shape[1]
    out_shape = [jax.ShapeDtypeStruct((n, wd), dt) for wd, dt in outs]
    out_specs = [pl.BlockSpec((tile, wd), lambda i: (i, 0)) for wd, _ in outs]
    return pl.pallas_call(
        functools.partial(_proj_kernel, plan=plan, tiles_a=tiles_a),
        grid=(n // tile,),
        in_specs=_two_part_specs(tiles_a, tile, d) + [
                  pl.BlockSpec((d, f), lambda i: (0, 0)),
                  pl.BlockSpec((1, f), lambda i: (0, 0)),
                  pl.BlockSpec((tile, LANES), lambda i: (pos_block(i), 0)),
                  pl.BlockSpec((tile, LANES), lambda i: (pos_block(i), 0))],
        out_specs=out_specs,
        out_shape=out_shape,
        compiler_params=_cparams(("parallel",)),
        name="proj",
    )(xa, xb, w, b, cos, sin)


def _projT_kernel(x_ref, wt_ref, o_ref, *, scale):
    acc = _nt(wt_ref[...], x_ref[...].astype(BF16))
    if scale != 1.0:
        acc = acc * scale
    o_ref[...] = acc.astype(o_ref.dtype)


def _project_t(x, wt, scale, dtype, tile):
    n, d = x.shape
    f = wt.shape[0]
    return pl.pallas_call(
        functools.partial(_projT_kernel, scale=scale),
        grid=(n // tile,),
        in_specs=[pl.BlockSpec((tile, d), lambda i: (i, 0)),
                  pl.BlockSpec((f, d), lambda i: (0, 0))],
        out_specs=pl.BlockSpec((f, tile), lambda i: (0, i)),
        out_shape=jax.ShapeDtypeStruct((f, n), dtype),
        compiler_params=_cparams(("parallel",)),
        name="proj_t",
    )(x, wt)


def _swa_prompt_kernel(sink_ref, q_ref, kp_ref, kc_ref, vp_ref, vc_ref, o_ref):
    n = pl.program_id(1)
    kband = jnp.concatenate([kp_ref[...], kc_ref[...]], axis=0)
    vband = jnp.concatenate([vp_ref[...], vc_ref[...]], axis=0)
    qi = lax.broadcasted_iota(jnp.int32, (BLOCK, 2 * BLOCK), 0)
    kj = lax.broadcasted_iota(jnp.int32, (BLOCK, 2 * BLOCK), 1)
    rel = qi - kj + BLOCK
    valid = (rel >= 0) & (rel < WINDOW) & ((n > 0) | (kj >= BLOCK))
    lane = lax.broadcasted_iota(jnp.int32, (BLOCK, LANES), 1)
    low = lane < HEAD_DIM
    for pair in range(N_HEADS // 2):
        slab = q_ref[:, pair * LANES:(pair + 1) * LANES]
        outs = []
        for half in range(2):
            p = 2 * pair + half
            qe = jnp.where(low if half == 0 else ~low, slab, jnp.zeros_like(slab))
            s = jnp.where(valid, _nt(qe, kband), NEG_INF)
            sink = sink_ref[p]
            m = jnp.maximum(jnp.max(s, axis=1, keepdims=True), sink)
            pr = jnp.exp(s - m)
            den = jnp.sum(pr, axis=1, keepdims=True) + jnp.exp(sink - m)
            outs.append(_nn(pr.astype(BF16), vband) / den)
        o_ref[:, pair * LANES:(pair + 1) * LANES] = jnp.where(low, outs[0], outs[1]).astype(o_ref.dtype)


def _swa_prompt(q, kb, vb, sinks, batch, seq):
    nb = seq // BLOCK
    cur = lambda b, n, s: (b * nb + n, 0)
    prev = lambda b, n, s: (b * nb + jnp.maximum(n - 1, 0), 0)
    return pl.pallas_call(
        _swa_prompt_kernel,
        grid_spec=pltpu.PrefetchScalarGridSpec(
            num_scalar_prefetch=1, grid=(batch, nb),
            in_specs=[pl.BlockSpec((BLOCK, D_MODEL), cur),
                      pl.BlockSpec((BLOCK, LANES), prev), pl.BlockSpec((BLOCK, LANES), cur),
                      pl.BlockSpec((BLOCK, LANES), prev), pl.BlockSpec((BLOCK, LANES), cur)],
            out_specs=pl.BlockSpec((BLOCK, D_MODEL), cur)),
        out_shape=jax.ShapeDtypeStruct((batch * seq, D_MODEL), BF16),
        compiler_params=_cparams(("parallel", "parallel")),
        name="swa_prompt",
    )(sinks, q, kb, kb, vb, vb)


def _swa_sample_kernel(q_ref, kw_ref, kn_ref, vw_ref, vn_ref, sink_ref, o_ref, *, nb, dec_seq):
    w = kw_ref.shape[2]
    rows = q_ref.shape[1]
    lane = lax.broadcasted_iota(jnp.int32, (LANES, w), 0)
    lowf = lane < HEAD_DIM
    lane_n = lax.broadcasted_iota(jnp.int32, (LANES, LANES), 1)
    lown = lane_n < HEAD_DIM
    tok = lax.broadcasted_iota(jnp.int32, (rows, w), 0) % dec_seq
    col = lax.broadcasted_iota(jnp.int32, (rows, w), 1)
    valid_w = col > tok + (w - WINDOW)
    tok_n = lax.broadcasted_iota(jnp.int32, (rows, LANES), 0) % dec_seq
    col_n = lax.broadcasted_iota(jnp.int32, (rows, LANES), 1)
    valid_n = (col_n <= tok_n) & (col_n < dec_seq)
    lane_o = lax.broadcasted_iota(jnp.int32, (rows, LANES), 1)
    for i in range(nb):
        x = q_ref[i]
        kw = kw_ref[i].astype(BF16)
        vw = vw_ref[i].astype(BF16)
        kn = kn_ref[i].astype(BF16)
        vn = vn_ref[i].astype(BF16)
        outs = []
        for half in range(2):
            fm = lowf if half == 0 else ~lowf
            nm = lown if half == 0 else ~lown
            sw = jnp.where(valid_w, _nn(x, jnp.where(fm, kw, jnp.zeros_like(kw))), NEG_INF)
            sn = jnp.where(valid_n, _nt(x, jnp.where(nm, kn, jnp.zeros_like(kn))), NEG_INF)
            sink = sink_ref[:, half:half + 1]
            m = jnp.maximum(jnp.maximum(jnp.max(sw, axis=1, keepdims=True),
                                        jnp.max(sn, axis=1, keepdims=True)), sink)
            pw = jnp.exp(sw - m)
            pn = jnp.exp(sn - m)
            den = jnp.sum(pw, axis=1, keepdims=True) + jnp.sum(pn, axis=1, keepdims=True) + jnp.exp(sink - m)
            acc = _nt(pw.astype(BF16), vw) + _nn(pn.astype(BF16), vn)
            outs.append(acc / den)
        o_ref[i] = jnp.where(lane_o < HEAD_DIM, outs[0], outs[1]).astype(o_ref.dtype)


def _swa_sample(qx, kwt, knew, vwt, vnew, sinkmat, dec_seq, nb=8):
    bd, rows, _ = qx.shape
    w = kwt.shape[2]
    blk = lambda s: pl.BlockSpec((nb,) + s, lambda i: (i, 0, 0))
    return pl.pallas_call(
        functools.partial(_swa_sample_kernel, nb=nb, dec_seq=dec_seq),
        grid=(bd // nb,),
        in_specs=[blk((rows, LANES)), blk((LANES, w)), blk((LANES, LANES)), blk((LANES, w)), blk((LANES, LANES)),
                  pl.BlockSpec((rows, LANES), lambda i: (0, 0))],
        out_specs=blk((rows, LANES)),
        out_shape=jax.ShapeDtypeStruct((bd, rows, LANES), F32),
        compiler_params=_cparams(("parallel",)),
        name="swa_sample",
    )(qx, kwt, knew, vwt, vnew, sinkmat)


def _layer_norm(y, g, b):
    mu = jnp.mean(y, axis=1, keepdims=True)
    d = y - mu
    var = jnp.mean(d * d, axis=1, keepdims=True)
    return d * lax.rsqrt(var + LN_EPS) * g + b


ROUTE_E1, ROUTE_E2, ROUTE_G1, ROUTE_G2, ROUTE_R1, ROUTE_R2 = range(6)


def _out_ln_router_kernel(oa_ref, ob_ref, xa_ref, xb_ref, wo_ref, g_ref, b_ref, rh_ref, rl_ref, rb_ref, tri_ref,
                          y_ref, route_ref, route_t_ref, count_ref, run_ref, *, o_tiles_a, x_tiles_a):
    @pl.when(pl.program_id(0) == 0)
    def _():
        run_ref[...] = jnp.zeros_like(run_ref)

    o = _two_part_load(oa_ref, ob_ref, o_tiles_a)
    x = _two_part_load(xa_ref, xb_ref, x_tiles_a)
    y = _layer_norm(DEEPNORM_ALPHA * x + _nn(o, wo_ref[...]), g_ref[...], b_ref[...])
    y_ref[...] = y
    yh = y.astype(BF16)
    yl = (y - yh.astype(F32)).astype(BF16)
    logits = _nn(yh, rh_ref[...]) + (_nn(yl, rh_ref[...]) + _nn(yh, rl_ref[...])) + rb_ref[...]
    lane = lax.broadcasted_iota(jnp.int32, logits.shape, 1)
    is_g = (lane >= N_EXPERTS) & (lane < N_EXPERTS + N_GROUPS)
    gl = jnp.where(is_g, logits, NEG_INF)
    gmax = jnp.max(gl, axis=1, keepdims=True)
    gsel = jnp.min(jnp.where(gl == gmax, lane, 2 * LANES), axis=1, keepdims=True) - N_EXPERTS
    gprob = 1.0 / jnp.sum(jnp.exp(gl - gmax), axis=1, keepdims=True)
    in_g = (lane >= gsel * EXPERTS_PER_GROUP) & (lane < (gsel + 1) * EXPERTS_PER_GROUP)
    el = jnp.where(in_g, logits, NEG_INF)
    v1 = jnp.max(el, axis=1, keepdims=True)
    i1 = jnp.min(jnp.where(el == v1, lane, 2 * LANES), axis=1, keepdims=True)
    el2 = jnp.where(lane == i1, NEG_INF, el)
    v2 = jnp.max(el2, axis=1, keepdims=True)
    i2 = jnp.min(jnp.where(el2 == v2, lane, 2 * LANES), axis=1, keepdims=True)
    e2 = jnp.exp(v2 - v1)
    den = 1.0 + e2
    g1 = (1.0 / den) * gprob
    g2 = (e2 / den) * gprob
    hit1 = lane == i1
    hit2 = lane == i2
    oh = jnp.where(hit1 | hit2, 1.0, 0.0)
    pos = run_ref[...] + _nn(tri_ref[...], oh.astype(BF16))
    r1 = jnp.sum(jnp.where(hit1, pos, 0.0), axis=1, keepdims=True)
    r2 = jnp.sum(jnp.where(hit2, pos, 0.0), axis=1, keepdims=True)
    run_ref[...] = run_ref[...] + jnp.sum(oh, axis=0, keepdims=True)
    count_ref[...] = run_ref[...]
    rec = jnp.zeros(logits.shape, F32)
    for col, val in ((ROUTE_E1, i1.astype(F32)), (ROUTE_E2, i2.astype(F32)), (ROUTE_G1, g1), (ROUTE_G2, g2),
                     (ROUTE_R1, r1), (ROUTE_R2, r2)):
        rec = jnp.where(lane == col, val, rec)
    route_ref[...] = rec
    route_t_ref[...] = rec.T[0:8, :]


def _out_ln_router(o_parts, x_parts, wo, g, b, rh, rl, rb, tile):
    oa, ob, o_tiles_a, n = _split_parts(o_parts, tile)
    xa, xb, x_tiles_a, _ = _split_parts(x_parts, tile)
    row = lambda i: (i, 0)
    fix = lambda i: (0, 0)
    tri = jnp.tril(jnp.ones((tile, tile), BF16), -1)
    return pl.pallas_call(
        functools.partial(_out_ln_router_kernel, o_tiles_a=o_tiles_a, x_tiles_a=x_tiles_a),
        grid=(n // tile,),
        in_specs=_two_part_specs(o_tiles_a, tile, D_MODEL) + _two_part_specs(x_tiles_a, tile, D_MODEL) + [
                  pl.BlockSpec((D_MODEL, D_MODEL), fix), pl.BlockSpec((1, D_MODEL), fix),
                  pl.BlockSpec((1, D_MODEL), fix), pl.BlockSpec((D_MODEL, LANES), fix),
                  pl.BlockSpec((D_MODEL, LANES), fix), pl.BlockSpec((1, LANES), fix),
                  pl.BlockSpec((tile, tile), fix)],
        out_specs=[pl.BlockSpec((tile, D_MODEL), row), pl.BlockSpec((tile, LANES), row),
                   pl.BlockSpec((8, tile), lambda i: (0, i)), pl.BlockSpec((1, LANES), fix)],
        out_shape=[jax.ShapeDtypeStruct((n, D_MODEL), F32), jax.ShapeDtypeStruct((n, LANES), F32),
                   jax.ShapeDtypeStruct((8, n), F32), jax.ShapeDtypeStruct((1, LANES), F32)],
        scratch_shapes=[pltpu.VMEM((1, LANES), F32)],
        compiler_params=_cparams(("arbitrary",)),
        name="out_ln_router",
    )(oa, ob, xa, xb, wo, g, b, rh, rl, rb, tri)


MOE_ROW_TILE = 512


def _row_copy(src_ref, src_row, dst_ref, dst_row, sem):
    return pltpu.make_async_copy(src_ref.at[pl.ds(src_row, 1), :], dst_ref.at[pl.ds(dst_row, 1), :], sem)


def _moe_scatter_kernel(zt_ref, nu_ref, dest_ref, y_ref, xs_ref, zbuf, sem, zsem, *, tm, n_tiles):
    t = y_ref.shape[0]

    @pl.when(pl.program_id(0) == 0)
    def _():
        zbuf[...] = jnp.zeros_like(zbuf)

        def fill(row0):
            return pltpu.make_async_copy(zbuf, xs_ref.at[pl.ds(pl.multiple_of(row0, tm), tm), :], zsem)

        for wait in (False, True):
            for e in range(N_EXPERTS):
                @pl.when(zt_ref[e] >= 0)
                def _(e=e, wait=wait):
                    fill(zt_ref[e]).wait() if wait else fill(zt_ref[e]).start()

            def tail(k, c, wait=wait):
                fill(k * tm).wait() if wait else fill(k * tm).start()
                return c

            lax.fori_loop(nu_ref[0], n_tiles, tail, 0)

    def start_row(r, c):
        for k in range(2):
            _row_copy(y_ref, r, xs_ref, dest_ref[0, 0, k * t + r], sem).start(priority=k)
        return c

    def wait_row(r, c):
        for k in range(2):
            _row_copy(y_ref, 0, xs_ref, 0, sem).wait()
        return c

    lax.fori_loop(0, t, start_row, 0, unroll=8)
    lax.fori_loop(0, t, wait_row, 0, unroll=8)


def _moe_scatter(y, dest, ztile, n_used, n_tiles, tile):
    n = y.shape[0]
    tm = MOE_ROW_TILE
    return pl.pallas_call(
        functools.partial(_moe_scatter_kernel, tm=tm, n_tiles=n_tiles),
        grid_spec=pltpu.PrefetchScalarGridSpec(
            num_scalar_prefetch=2, grid=(n // tile,),
            in_specs=[pl.BlockSpec((1, 1, 2 * tile), lambda i, zt, nu: (i, 0, 0), memory_space=pltpu.SMEM),
                      pl.BlockSpec((tile, D_MODEL), lambda i, zt, nu: (i, 0))],
            out_specs=pl.BlockSpec(memory_space=pl.ANY),
            scratch_shapes=[pltpu.VMEM((tm, D_MODEL), F32), pltpu.SemaphoreType.DMA(()),
                            pltpu.SemaphoreType.DMA(())]),
        out_shape=jax.ShapeDtypeStruct((n_tiles * tm, D_MODEL), F32),
        compiler_params=_cparams(("arbitrary",)),
        name="moe_scatter",
    )(ztile, n_used, dest, y)


def _moe_experts_kernel(te_ref, nu_ref, x_ref, wgu_ref, wdn_ref, o_ref):
    i = pl.program_id(0)

    @pl.when(i < nu_ref[0])
    def _():
        gu = _nn(x_ref[...].astype(BF16), wgu_ref[0, 0].astype(BF16))
        gate, up = gu[:, :D_FF_EXPERT], gu[:, D_FF_EXPERT:]
        h = (gate / (1.0 + jnp.exp(-gate))) * up
        o_ref[...] = _nn(h.astype(BF16), wdn_ref[0, 0].astype(BF16))

    @pl.when(i >= nu_ref[0])
    def _():
        o_ref[...] = jnp.zeros_like(o_ref)


def _moe_experts(xs, wgu, wdn, layer, tile_expert, n_used):
    tm = MOE_ROW_TILE
    n_tiles = xs.shape[0] // tm
    used = lambda i, nu: jnp.minimum(i, nu[0] - 1)
    return pl.pallas_call(
        _moe_experts_kernel,
        grid_spec=pltpu.PrefetchScalarGridSpec(
            num_scalar_prefetch=2, grid=(n_tiles,),
            in_specs=[pl.BlockSpec((tm, D_MODEL), lambda i, te, nu: (used(i, nu), 0)),
                      pl.BlockSpec((1, 1, D_MODEL, 2 * D_FF_EXPERT),
                                   lambda i, te, nu: (layer, te[used(i, nu)], 0, 0)),
                      pl.BlockSpec((1, 1, D_FF_EXPERT, D_MODEL),
                                   lambda i, te, nu: (layer, te[used(i, nu)], 0, 0))],
            out_specs=pl.BlockSpec((tm, D_MODEL), lambda i, te, nu: (i, 0))),
        out_shape=jax.ShapeDtypeStruct(xs.shape, F32),
        compiler_params=_cparams(("arbitrary",)),
        name="moe_experts",
    )(tile_expert, n_used, xs, wgu, wdn)


def _moe_combine_kernel(dcur_ref, dnext_ref, x_ref, route_ref, g_ref, b_ref, ys_ref, *rest, tiles_a):
    *o_refs, buf, sem = rest
    i = pl.program_id(0)
    t = x_ref.shape[0]
    slot = i % 2

    def gather(dref, slot_, wait):
        def body(r, c):
            for k in range(2):
                src = 0 if wait else dref[0, 0, k * t + r]
                cp = pltpu.make_async_copy(ys_ref.at[pl.ds(src, 1), :],
                                           buf.at[slot_, k, pl.ds(r, 1), :], sem.at[slot_])
                cp.wait() if wait else cp.start(priority=k)
            return c
        lax.fori_loop(0, t, body, 0, unroll=8)

    @pl.when(i == 0)
    def _():
        gather(dcur_ref, 0, False)

    @pl.when(i + 1 < pl.num_programs(0))
    def _():
        gather(dnext_ref, 1 - slot, False)

    gather(dcur_ref, slot, True)
    route = route_ref[...]
    lane = lax.broadcasted_iota(jnp.int32, route.shape, 1)
    g1 = jnp.sum(jnp.where(lane == ROUTE_G1, route, 0.0), axis=1, keepdims=True)
    g2 = jnp.sum(jnp.where(lane == ROUTE_G2, route, 0.0), axis=1, keepdims=True)
    y = g1 * buf[slot, 0] + g2 * buf[slot, 1]
    out = _layer_norm(DEEPNORM_ALPHA * x_ref[...] + y, g_ref[...], b_ref[...])
    if len(o_refs) == 1:
        o_refs[0][...] = out
    else:
        @pl.when(i < tiles_a)
        def _():
            o_refs[0][...] = out

        @pl.when(i >= tiles_a)
        def _():
            o_refs[1][...] = out


def _moe_combine(x, route, dest, ys, g, b, tile, split_at=None):
    n = x.shape[0]
    nt = n // tile
    row = lambda i: (i, 0)
    fix = lambda i: (0, 0)
    if split_at is None:
        tiles_a = nt
        out_specs = pl.BlockSpec((tile, D_MODEL), row)
        out_shape = jax.ShapeDtypeStruct((n, D_MODEL), F32)
    else:
        tiles_a = split_at // tile
        out_specs = [pl.BlockSpec((tile, D_MODEL), lambda i: (jnp.minimum(i, tiles_a - 1), 0)),
                     pl.BlockSpec((tile, D_MODEL), lambda i: (jnp.maximum(i - tiles_a, 0), 0))]
        out_shape = [jax.ShapeDtypeStruct((split_at, D_MODEL), F32),
                     jax.ShapeDtypeStruct((n - split_at, D_MODEL), F32)]
    return pl.pallas_call(
        functools.partial(_moe_combine_kernel, tiles_a=tiles_a),
        grid_spec=pltpu.PrefetchScalarGridSpec(
            num_scalar_prefetch=0, grid=(nt,),
            in_specs=[pl.BlockSpec((1, 1, 2 * tile), lambda i: (i, 0, 0), memory_space=pltpu.SMEM),
                      pl.BlockSpec((1, 1, 2 * tile), lambda i: (jnp.minimum(i + 1, nt - 1), 0, 0),
                                   memory_space=pltpu.SMEM),
                      pl.BlockSpec((tile, D_MODEL), row), pl.BlockSpec((tile, LANES), row),
                      pl.BlockSpec((1, D_MODEL), fix), pl.BlockSpec((1, D_MODEL), fix),
                      pl.BlockSpec(memory_space=pl.ANY)],
            out_specs=out_specs,
            scratch_shapes=[pltpu.VMEM((2, 2, tile, D_MODEL), F32), pltpu.SemaphoreType.DMA((2,))]),
        out_shape=out_shape,
        compiler_params=_cparams(("arbitrary",)),
        name="moe_combine",
    )(dest, dest, x, route, g, b, ys)


def _moe_dest_kernel(off_ref, rt_ref, o_ref):
    rec = rt_ref[...].astype(jnp.int32)
    base = jnp.zeros_like(rec)
    for e in range(N_EXPERTS):
        base = jnp.where(rec == e, off_ref[e], base)
    o_ref[...] = base + pltpu.roll(rec, 8 - (ROUTE_R1 - ROUTE_E1), axis=0)


def _moe_dest(route_t, offs, tile):
    n = route_t.shape[1]
    return pl.pallas_call(
        _moe_dest_kernel,
        grid_spec=pltpu.PrefetchScalarGridSpec(
            num_scalar_prefetch=1, grid=(n // tile,),
            in_specs=[pl.BlockSpec((8, tile), lambda i, off: (0, i))],
            out_specs=pl.BlockSpec((8, tile), lambda i, off: (0, i))),
        out_shape=jax.ShapeDtypeStruct((8, n), jnp.int32),
        compiler_params=_cparams(("parallel",)),
        name="moe_dest",
    )(offs, route_t)


def _tile_dest(dest8, tile):
    return dest8[0:2].reshape(2, -1, tile).transpose(1, 0, 2).reshape(-1, 1, 2 * tile)


def _sparse_moe(y, route, route_t, counts, wgu, wdn, layer, g, b, tile, combine_tile, split_at=None):
    n = y.shape[0]
    tm = MOE_ROW_TILE
    n_tiles = -(-(2 * n + N_EXPERTS * (tm - 1)) // tm)
    cnt = counts[0, :N_EXPERTS].astype(jnp.int32)
    padded = ((cnt + tm - 1) // tm) * tm
    ends = jnp.cumsum(padded)
    offs = ends - padded
    n_used = (ends[-1] // tm).reshape(1)
    tile_row0 = jnp.arange(n_tiles, dtype=jnp.int32) * tm
    tile_expert = jnp.minimum(jnp.sum(tile_row0[:, None] >= ends[None, :], axis=1), N_EXPERTS - 1).astype(jnp.int32)
    ztile = jnp.where(padded > 0, ends - tm, -1).astype(jnp.int32)
    dest8 = _moe_dest(route_t, offs, tile)
    xs = _moe_scatter(y, _tile_dest(dest8, tile), ztile, n_used, n_tiles, tile)
    ys = _moe_experts(xs, wgu, wdn, layer, tile_expert, n_used)
    return _moe_combine(y, route, _tile_dest(dest8, combine_tile), ys, g, b, combine_tile, split_at)


def _reduce0(x, op):
    r, w = x.shape
    g = next(g for g in (64, 32, 16, 8) if r % g == 0)
    return op(op(x.reshape(r // g, g, w), axis=0), axis=0, keepdims=True)


def _count(mask, axis):
    ones = jnp.where(mask, 1.0, 0.0)
    if axis == 0:
        return _reduce0(ones, jnp.sum)
    return jnp.sum(ones, axis=axis, keepdims=True)


def _order_key(x):
    bits = lax.bitcast_convert_type(x, jnp.int32)
    return jnp.where(bits < 0, bits ^ 0x7FFFFFFF, bits)


def _topk_mask(load_key, idx, k, axis, idx_bits):
    kf = float(k)
    r = jnp.where(_count(load_key() >= 0, axis) >= kf, 0, INT_MIN).astype(jnp.int32)

    def value_bit(i, r):
        cand = r | jnp.left_shift(jnp.int32(1), 30 - i)
        return jnp.where(_count(load_key() >= cand, axis) >= kf, cand, r)

    r = lax.fori_loop(0, 31, value_bit, r)
    need = kf - _count(load_key() > r, axis)
    n_tie = _count(load_key() == r, axis)

    def index_bit(i, j):
        cand = j | jnp.left_shift(jnp.int32(1), idx_bits - 1 - i)
        return jnp.where(_count((load_key() == r) & (idx < cand), axis) < need, cand, j)

    some_left_out = jnp.max(jnp.where(need < n_tie, 1.0, 0.0)) > 0.0
    j = lax.cond(some_left_out,
                 lambda: lax.fori_loop(0, idx_bits, index_bit, jnp.zeros_like(r)),
                 lambda: jnp.full_like(r, (1 << idx_bits) - 1))
    key = load_key()
    return (key > r) | ((key == r) & (idx <= j))


def _dsa_prompt_kernel(q_ref, kb_ref, vb_ref, kid_ref, qi_ref, wit_ref, o_ref, key_ref,
                       *, classes, topk, slot_kv):
    n = pl.program_id(1)
    for n0, cnt in classes:
        @pl.when((n >= n0) & (n < n0 + cnt))
        def _(n0=n0, cnt=cnt):
            _dsa_prompt_block(q_ref, kb_ref, vb_ref, kid_ref, qi_ref, wit_ref, o_ref, key_ref,
                              sk=(n0 + cnt) * BLOCK, topk=topk, slot_kv=slot_kv)


def _dsa_prompt_block(q_ref, kb_ref, vb_ref, kid_ref, qi_ref, wit_ref, o_ref, key_ref, *, sk, topk, slot_kv):
    t0 = pl.program_id(1) * BLOCK
    lane = lax.broadcasted_iota(jnp.int32, (BLOCK, LANES), 1)
    low = lane < HEAD_DIM
    s_idx = lax.broadcasted_iota(jnp.int32, (sk, BLOCK), 0)
    causal = s_idx <= t0 + lax.broadcasted_iota(jnp.int32, (sk, BLOCK), 1)

    kid = kid_ref[0:sk, :]
    sc = jnp.zeros((sk, BLOCK), F32)
    for hp in range(IDX_HEADS // 2):
        slab = qi_ref[:, hp * LANES:(hp + 1) * LANES]
        z = jnp.zeros_like(slab)
        qh2 = jnp.concatenate([jnp.where(low, slab, z), jnp.where(low, z, slab)], axis=0)
        d = jnp.maximum(_nt(kid, qh2), 0.0)
        sc = sc + d[:, :BLOCK] * wit_ref[2 * hp:2 * hp + 1, :] + d[:, BLOCK:] * wit_ref[2 * hp + 1:2 * hp + 2, :]
    key_ref[0:sk, :] = _order_key(jnp.where(causal, sc, NEG_INF))
    sel = _topk_mask(lambda: key_ref[0:sk, :], s_idx, topk, 0, (sk - 1).bit_length()) & causal
    bias = jnp.where(sel, 0.0, NEG_INF).T

    kb = kb_ref[0:sk, :]
    vb = vb_ref[0:sk, :]
    res = [None] * N_HEADS
    for kv in range(KV_HEADS_B):
        slots = [p for p in range(N_HEADS) if slot_kv[p] == kv]
        qes = []
        for p in slots:
            slab = q_ref[:, (p // 2) * LANES:(p // 2 + 1) * LANES]
            z = jnp.zeros_like(slab)
            qm = jnp.where(low, slab, z) if p % 2 == 0 else jnp.where(low, z, slab)
            qes.append(jnp.concatenate([qm, z] if kv // 2 == 0 else [z, qm], axis=1))
        s_all = _nt(jnp.concatenate(qes, axis=0), kb)
        prs, dens = [], []
        for i in range(len(slots)):
            s = s_all[i * BLOCK:(i + 1) * BLOCK, :] + bias
            pr = jnp.exp2(s - jnp.max(s, axis=1, keepdims=True))
            dens.append(jnp.sum(pr, axis=1, keepdims=True))
            prs.append(pr.astype(BF16))
        o_all = _nn(jnp.concatenate(prs, axis=0), vb)
        for i, p in enumerate(slots):
            res[p] = o_all[i * BLOCK:(i + 1) * BLOCK, (kv // 2) * LANES:(kv // 2 + 1) * LANES] / dens[i]
    for pair in range(N_HEADS // 2):
        o_ref[:, pair * LANES:(pair + 1) * LANES] = jnp.where(low, res[2 * pair], res[2 * pair + 1]).astype(o_ref.dtype)


def _dsa_prompt(q, kb, vb, kid, qi, wit, batch, seq, topk, slot_kv, n_classes=4):
    nb = seq // BLOCK
    per = -(-nb // n_classes)
    classes = tuple((n0, min(per, nb - n0)) for n0 in range(0, nb, per))
    qmap = lambda b, n: (b * nb + n, 0)
    return pl.pallas_call(
        functools.partial(_dsa_prompt_kernel, classes=classes, topk=topk, slot_kv=slot_kv),
        grid=(batch, nb),
        in_specs=[pl.BlockSpec((BLOCK, D_MODEL), qmap),
                  pl.BlockSpec((seq, 2 * LANES), lambda b, n: (b, 0)),
                  pl.BlockSpec((seq, 2 * LANES), lambda b, n: (b, 0)),
                  pl.BlockSpec((seq, LANES), lambda b, n: (b, 0)),
                  pl.BlockSpec((BLOCK, IDX_HEADS * IDX_DIM), qmap),
                  pl.BlockSpec((IDX_HEADS, BLOCK), lambda b, n: (0, b * nb + n))],
        out_specs=pl.BlockSpec((BLOCK, D_MODEL), qmap),
        out_shape=jax.ShapeDtypeStruct((batch * seq, D_MODEL), BF16),
        scratch_shapes=[pltpu.VMEM((seq, BLOCK), jnp.int32)],
        compiler_params=_cparams(("parallel", "arbitrary")),
        name="dsa_prompt",
    )(q, kb, vb, kid, qi, wit)


def _page_copies(pt_ref, pool_ref, buf_ref, sem, step, slot, pairb, page0, n_pages):
    cps = []
    for bi in range(pairb):
        for i in range(n_pages):
            page = pt_ref[step * pairb + bi, page0 + i]
            cps.append(pltpu.make_async_copy(
                pool_ref.at[page], buf_ref.at[slot, bi, :, pl.ds(i * PAGE_SIZE, PAGE_SIZE)], sem.at[slot]))
    return cps


def _dsa_sample_scores_kernel(pt_ref, pool_ref, qi_ref, w_ref, kin_ref, o_ref, buf_ref, sem, *, pairb, n_pages, dec_seq):
    g = pl.program_id(0)
    ng = pl.num_programs(0)
    slot = g % 2

    @pl.when(g == 0)
    def _():
        for cp in _page_copies(pt_ref, pool_ref, buf_ref, sem, 0, 0, pairb, 0, n_pages):
            cp.start()

    @pl.when(g + 1 < ng)
    def _():
        for cp in _page_copies(pt_ref, pool_ref, buf_ref, sem, g + 1, 1 - slot, pairb, 0, n_pages):
            cp.start()

    for cp in _page_copies(pt_ref, pool_ref, buf_ref, sem, g, slot, pairb, 0, n_pages):
        cp.wait()

    past = n_pages * PAGE_SIZE
    lane = lax.broadcasted_iota(jnp.int32, (1, LANES), 1)
    for bi in range(pairb):
        qi = qi_ref[bi]
        w = w_ref[bi]
        r = jnp.maximum(_nn(qi, buf_ref[slot, bi].astype(BF16)), 0.0) * w
        rn = jnp.maximum(_nt(qi, kin_ref[bi]), 0.0) * w
        for t in range(dec_seq):
            row = bi * dec_seq + t
            o_ref[row:row + 1, 0:past] = jnp.sum(r[t * IDX_HEADS:(t + 1) * IDX_HEADS], axis=0, keepdims=True)
            new = jnp.sum(rn[t * IDX_HEADS:(t + 1) * IDX_HEADS], axis=0, keepdims=True)
            o_ref[row:row + 1, past:past + LANES] = jnp.where(lane <= t, new, NEG_INF)


def _dsa_sample_scores(page_table, pool_t, qix, wcol, kin, dec_seq, pairb=2):
    bd, n_pages = page_table.shape
    past = n_pages * PAGE_SIZE
    rows = dec_seq * IDX_HEADS
    blk = lambda s: pl.BlockSpec((pairb,) + s, lambda g, pt: (g, 0, 0))
    return pl.pallas_call(
        functools.partial(_dsa_sample_scores_kernel, pairb=pairb, n_pages=n_pages, dec_seq=dec_seq),
        grid_spec=pltpu.PrefetchScalarGridSpec(
            num_scalar_prefetch=1, grid=(bd // pairb,),
            in_specs=[pl.BlockSpec(memory_space=pl.ANY), blk((rows, IDX_DIM)), blk((rows, 1)), blk((LANES, IDX_DIM))],
            out_specs=pl.BlockSpec((pairb * dec_seq, past + LANES), lambda g, pt: (g, 0)),
            scratch_shapes=[pltpu.VMEM((2, pairb, IDX_DIM, past), F32), pltpu.SemaphoreType.DMA((2,))]),
        out_shape=jax.ShapeDtypeStruct((bd * dec_seq, past + LANES), F32),
        compiler_params=_cparams(("arbitrary",)),
        name="dsa_sample_scores",
    )(page_table, pool_t, qix, wcol, kin)


def _select_bias_kernel(sc_ref, o_ref, key_ref, *, topk):
    idx = lax.broadcasted_iota(jnp.int32, sc_ref.shape, 1)
    key_ref[...] = _order_key(sc_ref[...])
    sel = _topk_mask(lambda: key_ref[...], idx, topk, 1, (sc_ref.shape[1] - 1).bit_length())
    o_ref[...] = jnp.where(sel & (sc_ref[...] > NEG_INF), 0.0, NEG_INF)


def _select_bias(sc, topk, tile):
    r, c = sc.shape
    return pl.pallas_call(
        functools.partial(_select_bias_kernel, topk=topk),
        grid=(r // tile,),
        in_specs=[pl.BlockSpec((tile, c), lambda i: (i, 0))],
        out_specs=pl.BlockSpec((tile, c), lambda i: (i, 0)),
        out_shape=jax.ShapeDtypeStruct((r, c), F32),
        scratch_shapes=[pltpu.VMEM((tile, c), jnp.int32)],
        compiler_params=_cparams(("parallel",)),
        name="select_bias",
    )(sc)


def _dsa_sample_attn_kernel(pt_ref, kpool_ref, vpool_ref, qe_ref, bias_ref, biasn_ref, kn_ref, vn_ref, o_ref,
                            kbuf, vbuf, ksem, vsem, m_ref, l_ref, acc_ref, *, pairb, chunk_pages, dec_seq):
    g = pl.program_id(0)
    c = pl.program_id(1)
    nch = pl.num_programs(1)
    lin = g * nch + c
    total = pl.num_programs(0) * nch
    slot = lin % 2

    def copies(step_lin, slot_):
        gg = step_lin // nch
        cc = step_lin % nch
        return (_page_copies(pt_ref, kpool_ref, kbuf, ksem, gg, slot_, pairb, cc * chunk_pages, chunk_pages)
                + _page_copies(pt_ref, vpool_ref, vbuf, vsem, gg, slot_, pairb, cc * chunk_pages, chunk_pages))

    @pl.when(lin == 0)
    def _():
        for cp in copies(0, 0):
            cp.start()

    @pl.when(lin + 1 < total)
    def _():
        for cp in copies(lin + 1, 1 - slot):
            cp.start()

    for cp in copies(lin, slot):
        cp.wait()

    @pl.when(c == 0)
    def _():
        m_ref[...] = jnp.full_like(m_ref, -1e30)
        l_ref[...] = jnp.zeros_like(l_ref)
        acc_ref[...] = jnp.zeros_like(acc_ref)

    rows = qe_ref.shape[1]
    per_tok = rows // dec_seq

    def expand(b4):
        return jnp.concatenate(
            [jnp.broadcast_to(b4[t:t + 1, :], (per_tok, b4.shape[1])) for t in range(dec_seq)], axis=0)

    def update(bi, s, pv):
        m_old = m_ref[bi]
        m_new = jnp.maximum(m_old, jnp.max(s, axis=1, keepdims=True))
        alpha = jnp.exp2(m_old - m_new)
        pr = jnp.exp2(s - m_new)
        l_ref[bi] = alpha * l_ref[bi] + jnp.sum(pr, axis=1, keepdims=True)
        acc_ref[bi] = alpha * acc_ref[bi] + pv(pr.astype(BF16))
        m_ref[bi] = m_new

    for bi in range(pairb):
        qe = qe_ref[bi]
        kt = kbuf[slot, bi].astype(BF16)
        vt = vbuf[slot, bi].astype(BF16)
        s = _nn(qe, kt) + expand(bias_ref[bi * dec_seq:(bi + 1) * dec_seq, :])
        update(bi, s, lambda pr: _nt(pr, vt))

    @pl.when(c == nch - 1)
    def _():
        for bi in range(pairb):
            qe = qe_ref[bi]
            s = _nt(qe, kn_ref[bi]) + expand(biasn_ref[bi * dec_seq:(bi + 1) * dec_seq, :])
            update(bi, s, lambda pr: _nn(pr, vn_ref[bi]))
            o_ref[bi] = acc_ref[bi] / l_ref[bi]


def _dsa_sample_attn(page_table, kpool_t, vpool_t, qe, bias, knew, vnew, dec_seq, pairb=2, chunk_pages=16):
    bd, n_pages = page_table.shape
    nch = n_pages // chunk_pages
    chunk = chunk_pages * PAGE_SIZE
    rows = qe.shape[1]
    kvw = qe.shape[2]
    blk = lambda s: pl.BlockSpec((pairb,) + s, lambda g, c, pt: (g, 0, 0))
    return pl.pallas_call(
        functools.partial(_dsa_sample_attn_kernel, pairb=pairb, chunk_pages=chunk_pages, dec_seq=dec_seq),
        grid_spec=pltpu.PrefetchScalarGridSpec(
            num_scalar_prefetch=1, grid=(bd // pairb, nch),
            in_specs=[pl.BlockSpec(memory_space=pl.ANY), pl.BlockSpec(memory_space=pl.ANY),
                      blk((rows, kvw)),
                      pl.BlockSpec((pairb * dec_seq, chunk), lambda g, c, pt: (g, c)),
                      pl.BlockSpec((pairb * dec_seq, LANES), lambda g, c, pt: (g, n_pages)),
                      blk((LANES, kvw)), blk((LANES, kvw))],
            out_specs=blk((rows, kvw)),
            scratch_shapes=[pltpu.VMEM((2, pairb, kvw, chunk), F32), pltpu.VMEM((2, pairb, kvw, chunk), F32),
                            pltpu.SemaphoreType.DMA((2,)), pltpu.SemaphoreType.DMA((2,)),
                            pltpu.VMEM((pairb, rows, 1), F32), pltpu.VMEM((pairb, rows, 1), F32),
                            pltpu.VMEM((pairb, rows, kvw), F32)]),
        out_shape=jax.ShapeDtypeStruct((bd, rows, kvw), F32),
        compiler_params=_cparams(("arbitrary", "arbitrary")),
        name="dsa_sample_attn",
    )(page_table, kpool_t, vpool_t, qe, bias, bias, knew, vnew)


def _pick_tile(n, candidates):
    for t in candidates:
        if n % t == 0:
            return t
    raise ValueError(f"no token tile for {n}")


def _hi_lo(w):
    hi = w.astype(BF16)
    return hi, (w - hi.astype(F32)).astype(BF16)


def _pad_rows(x, rows):
    return jnp.pad(x, ((0, 0), (0, rows - x.shape[1]), (0, 0)))


def _post_attention(o_parts, x_parts, w_o_phys, l, moe, ln, tile, moe_tile, split_at=None):
    (w_rg, b_rg, w_re, b_re, w_gu, w_dn) = moe
    (ln1_g, ln1_b, ln2_g, ln2_b) = ln
    rw = jnp.zeros((D_MODEL, LANES), F32).at[:, :N_EXPERTS].set(w_re[l])
    rw = rw.at[:, N_EXPERTS:N_EXPERTS + N_GROUPS].set(w_rg[l])
    rb = jnp.zeros((1, LANES), F32).at[0, :N_EXPERTS].set(b_re[l]).at[0, N_EXPERTS:N_EXPERTS + N_GROUPS].set(b_rg[l])
    rh, rl = _hi_lo(rw)
    y, route, route_t, counts = _out_ln_router(o_parts, x_parts, w_o_phys.astype(BF16), ln1_g[l].reshape(1, -1),
                                               ln1_b[l].reshape(1, -1), rh, rl, rb, tile)
    return _sparse_moe(y, route, route_t, counts, w_gu, w_dn, l, ln2_g[l].reshape(1, -1),
                       ln2_b[l].reshape(1, -1), tile, moe_tile, split_at)


def kernel(x_prompt, x_sample, cache_win_k, cache_win_v, cache_k, cache_v, cache_idx_k, page_table,
           a_w_qkv, a_b_qkv, a_sinks, a_w_o, b_w_in, b_w_o,
           moe_w_rg, moe_b_rg, moe_w_re, moe_b_re, moe_w_gu, moe_w_dn,
           ln1_g, ln1_b, ln2_g, ln2_b):
    B, S, _ = x_prompt.shape
    Bd, T, _ = x_sample.shape
    n_p, n_s = B * S, Bd * T
    n_tot = n_p + n_s
    past = page_table.shape[1] * PAGE_SIZE
    tile = _pick_tile(math.gcd(S, n_s), (512, 256, 128))
    moe_tile = _pick_tile(tile, (256, 128))
    qd = N_HEADS * HEAD_DIM
    moe = (moe_w_rg, moe_b_rg, moe_w_re, moe_b_re, moe_w_gu, moe_w_dn)
    ln = (ln1_g, ln1_b, ln2_g, ln2_b)

    x_parts = (x_prompt.reshape(n_p, D_MODEL), x_sample.reshape(n_s, D_MODEL))
    pos = jnp.concatenate([jnp.arange(S, dtype=jnp.int32), past + jnp.arange(tile, dtype=jnp.int32) % T])
    cos, sin = _rope_tables(pos)
    pos_block = lambda i: jnp.where(i < n_p // tile, i % (S // tile), S // tile)
    scale = HEAD_DIM ** -0.5

    a = 0
    kva = KV_HEADS_A * HEAD_DIM
    order_a = _slot_order(KV_HEADS_A)
    perm_a = _col_perm(order_a)
    wq, wk, wv = a_w_qkv[a][:, :qd], a_w_qkv[a][:, qd:qd + kva], a_w_qkv[a][:, qd + kva:]
    bq, bk, bv = a_b_qkv[a][:qd], a_b_qkv[a][qd:qd + kva], a_b_qkv[a][qd + kva:]
    w_a = jnp.concatenate([wq[:, perm_a], wk, wv], axis=1).astype(BF16)
    b_a = jnp.concatenate([bq[perm_a], bk, bv]).reshape(1, -1)
    plan_a = ((0, 8, True, scale, (0,)), (8, 1, True, 1.0, (1, 2)), (9, 1, False, 1.0, (3, 4)))
    q0, k0, k0b, v0, v0b = _project(x_parts, w_a, b_a, cos, sin, pos_block, plan_a,
                                    [(qd, BF16), (kva, F32), (kva, BF16), (kva, F32), (kva, BF16)], tile)
    sinks_phys = a_sinks[a][np.asarray(order_a)]
    o_p = _swa_prompt(q0, k0b, v0b, sinks_phys, B, S)

    w_win = cache_win_k.shape[2]
    qx = q0[n_p:].reshape(Bd, T, N_HEADS // 2, LANES).transpose(0, 2, 1, 3).reshape(Bd, T * N_HEADS // 2, LANES)
    kwt = cache_win_k[a].transpose(0, 2, 3, 1).reshape(Bd, kva, w_win)
    vwt = cache_win_v[a].transpose(0, 2, 3, 1).reshape(Bd, kva, w_win)
    k0s = k0[n_p:].reshape(Bd, T, kva)
    v0s = v0[n_p:].reshape(Bd, T, kva)
    sinkmat = jnp.zeros((T * N_HEADS // 2, LANES), F32).at[:, :2].set(
        jnp.repeat(sinks_phys.reshape(N_HEADS // 2, 2), T, axis=0))
    ox = _swa_sample(qx, kwt, _pad_rows(k0s, LANES), vwt, _pad_rows(v0s, LANES), sinkmat, T,
                     nb=_pick_tile(Bd, (8, 4, 2, 1)))
    o_s = ox.reshape(Bd, N_HEADS // 2, T, LANES).transpose(0, 2, 1, 3).reshape(n_s, qd).astype(BF16)

    x = _post_attention((o_p, o_s), x_parts, a_w_o[a][perm_a, :], 0, moe, ln, tile, moe_tile)

    wp = min(WINDOW, S)
    win_k_prompt = k0[:n_p].reshape(B, S, KV_HEADS_A, HEAD_DIM)[:, S - wp:][None]
    win_v_prompt = v0[:n_p].reshape(B, S, KV_HEADS_A, HEAD_DIM)[:, S - wp:][None]
    win_k_sample = jnp.concatenate([cache_win_k[a], k0s.reshape(Bd, T, KV_HEADS_A, HEAD_DIM)], axis=1)[:, -w_win:][None]
    win_v_sample = jnp.concatenate([cache_win_v[a], v0s.reshape(Bd, T, KV_HEADS_A, HEAD_DIM)], axis=1)[:, -w_win:][None]

    bl = 0
    kvb = KV_HEADS_B * HEAD_DIM
    qid = IDX_HEADS * IDX_DIM
    order_b = _slot_order(KV_HEADS_B)
    perm_b = _col_perm(order_b)
    slot_kv = tuple(h // (N_HEADS // KV_HEADS_B) for h in order_b)
    w_in = b_w_in[bl]
    c0 = 0
    wq = w_in[:, c0:c0 + qd]; c0 += qd
    wk = w_in[:, c0:c0 + kvb]; c0 += kvb
    wv = w_in[:, c0:c0 + kvb]; c0 += kvb
    wqi = w_in[:, c0:c0 + qid]; c0 += qid
    wki = w_in[:, c0:c0 + IDX_DIM]; c0 += IDX_DIM
    wwi = w_in[:, c0:c0 + IDX_HEADS]
    w_b = jnp.concatenate([wq[:, perm_b], wk, wv, wqi, wki, wki], axis=1).astype(BF16)
    b_b = jnp.zeros((1, w_b.shape[1]), F32)
    plan_b = ((0, 8, True, scale * math.log2(math.e), (0,)), (8, 2, True, 1.0, (1, 2)), (10, 2, False, 1.0, (3, 7)),
              (12, 4, True, IDX_DIM ** -0.5, (4,)), (16, 1, True, 1.0, (5, 6)))
    q1, k1, k1b, v1, qi1, ki1, kid1, v1b = _project(
        (x,), w_b, b_b, cos, sin, pos_block, plan_b,
        [(qd, BF16), (kvb, F32), (kvb, BF16), (kvb, F32), (qid, BF16), (LANES, F32), (LANES, BF16), (kvb, BF16)],
        tile)
    wwi_t = jnp.zeros((2 * IDX_HEADS, D_MODEL), F32).at[:IDX_HEADS].set(wwi.T).astype(BF16)
    wit1 = _project_t(x, wwi_t, IDX_HEADS ** -0.5, F32, tile)

    o_p = _dsa_prompt(q1, k1b, v1b, kid1, qi1, wit1, B, S, min(TOPK_MAX, S // 4), slot_kv)

    n_pool = cache_k.shape[1]
    pool_ik_t = cache_idx_k[bl].transpose(0, 2, 1)
    qix = qi1[n_p:].reshape(Bd, T * IDX_HEADS, IDX_DIM)
    wcol = wit1[:IDX_HEADS, n_p:].T.reshape(Bd, T * IDX_HEADS, 1)
    kin = _pad_rows(kid1[n_p:, :IDX_DIM].reshape(Bd, T, IDX_DIM), LANES)
    sc = _dsa_sample_scores(page_table, pool_ik_t, qix, wcol, kin, T)
    bias = _select_bias(sc, min(TOPK_MAX, (past + T) // 4), _pick_tile(n_s, (128, 64, 32, 16, 8)))
    kpool_t = cache_k[bl].transpose(0, 2, 3, 1).reshape(n_pool, kvb, PAGE_SIZE)
    vpool_t = cache_v[bl].transpose(0, 2, 3, 1).reshape(n_pool, kvb, PAGE_SIZE)
    onehot = (np.asarray(slot_kv)[:, None] == np.arange(KV_HEADS_B)[None, :]).astype(np.float32)
    qe = (q1[n_p:].reshape(n_s, N_HEADS, 1, HEAD_DIM) * jnp.asarray(onehot, BF16)[None, :, :, None])
    qe = qe.reshape(Bd, T * N_HEADS, kvb)
    knew = _pad_rows(k1b[n_p:].reshape(Bd, T, kvb), LANES)
    vnew = _pad_rows(v1b[n_p:].reshape(Bd, T, kvb), LANES)
    ox = _dsa_sample_attn(page_table, kpool_t, vpool_t, qe, bias, knew, vnew, T,
                          chunk_pages=_pick_tile(page_table.shape[1], (16, 8, 4, 2, 1)))
    ox = ox.reshape(Bd, T, N_HEADS, KV_HEADS_B, HEAD_DIM)
    o_s = ox[:, :, np.arange(N_HEADS), np.asarray(slot_kv), :].reshape(n_s, qd).astype(BF16)

    y_p, y_s = _post_attention((o_p, o_s), (x,), b_w_o[bl][perm_b, :], 1, moe, ln, tile, moe_tile, split_at=n_p)

    k_prompt = k1[:n_p].reshape(B, S, KV_HEADS_B, HEAD_DIM)[None]
    v_prompt = v1[:n_p].reshape(B, S, KV_HEADS_B, HEAD_DIM)[None]
    idx_k_prompt = ki1[:n_p, :IDX_DIM].reshape(B, S, IDX_DIM)[None]
    k_sample = k1[n_p:].reshape(Bd, T, KV_HEADS_B, HEAD_DIM)[None]
    v_sample = v1[n_p:].reshape(Bd, T, KV_HEADS_B, HEAD_DIM)[None]
    idx_k_sample = ki1[n_p:, :IDX_DIM].reshape(Bd, T, IDX_DIM)[None]

    return (y_p.reshape(B, S, D_MODEL), y_s.reshape(Bd, T, D_MODEL),
            win_k_prompt, win_v_prompt, win_k_sample, win_v_sample,
            k_prompt, v_prompt, idx_k_prompt, k_sample, v_sample, idx_k_sample)
```

```python
import functools
import math

import jax
import jax.numpy as jnp
import numpy as np
from jax import lax
from jax.experimental import pallas as pl
from jax.experimental.pallas import tpu as pltpu

D_MODEL = 1024
HEAD_DIM = 64
N_HEADS = 16
KV_HEADS_A = 2
KV_HEADS_B = 4
WINDOW = 128
BLOCK = 128
IDX_HEADS = 8
IDX_DIM = 64
TOPK_MAX = 256
N_GROUPS = 4
EXPERTS_PER_GROUP = 8
N_EXPERTS = N_GROUPS * EXPERTS_PER_GROUP
D_FF_EXPERT = 256
ROPE_THETA = 10000.0
LN_EPS = 1e-5
DEPTH = 2
DEEPNORM_ALPHA = (2 * DEPTH) ** 0.25
PAST_LEN = 8192
PAGE_SIZE = 128

LANES = 128
NEG_INF = float("-inf")
INT_MIN = -(2 ** 31)
VMEM_LIMIT = 56 * 1024 * 1024

F32 = jnp.float32
BF16 = jnp.bfloat16


def _slot_order(kv_heads):
    g = N_HEADS // kv_heads
    even = [h for h in range(N_HEADS) if (h // g) % 2 == 0]
    odd = [h for h in range(N_HEADS) if (h // g) % 2 == 1]
    order = []
    for a, b in zip(even, odd):
        order += [a, b]
    return order


def _col_perm(order):
    return np.concatenate([np.arange(h * HEAD_DIM, (h + 1) * HEAD_DIM) for h in order])


def _rope_tables(pos):
    half = HEAD_DIM // 2
    inv = jnp.exp(jnp.arange(half, dtype=F32) * (-2.0 * math.log(ROPE_THETA) / HEAD_DIM))
    ang = pos.astype(F32)[:, None] * inv[None, :]
    c, s = jnp.cos(ang), jnp.sin(ang)
    return jnp.concatenate([c, c, c, c], axis=1), jnp.concatenate([-s, s, -s, s], axis=1)


def _rope_slab(x, cos, sin, lane):
    swapped = jnp.where((lane % HEAD_DIM) < HEAD_DIM // 2,
                        pltpu.roll(x, LANES - HEAD_DIM // 2, axis=1),
                        pltpu.roll(x, HEAD_DIM // 2, axis=1))
    return x * cos + swapped * sin


def _nt(a, b):
    return lax.dot_general(a, b, (((1,), (1,)), ((), ())), preferred_element_type=F32)


def _nn(a, b):
    return jnp.dot(a, b, preferred_element_type=F32)


def _cparams(sem):
    return pltpu.CompilerParams(dimension_semantics=sem, vmem_limit_bytes=VMEM_LIMIT)


def _two_part_specs(tiles_a, tile, width, nargs=1):
    return [pl.BlockSpec((tile, width), lambda i, *_: (jnp.minimum(i, tiles_a - 1), 0)),
            pl.BlockSpec((tile, width), lambda i, *_: (jnp.maximum(i - tiles_a, 0), 0))]


def _two_part_load(a_ref, b_ref, tiles_a):
    return jnp.where(pl.program_id(0) < tiles_a, a_ref[...], b_ref[...])


def _split_parts(parts, tile):
    if len(parts) == 1:
        return parts[0], parts[0], parts[0].shape[0] // tile, parts[0].shape[0]
    a, b = parts
    assert a.shape[0] % tile == 0 and b.shape[0] % tile == 0
    return a, b, a.shape[0] // tile, a.shape[0] + b.shape[0]


def _proj_kernel(xa_ref, xb_ref, w_ref, b_ref, cos_ref, sin_ref, *out_refs, plan, tiles_a):
    xb = _two_part_load(xa_ref, xb_ref, tiles_a).astype(BF16)
    cos = cos_ref[...]
    sin = sin_ref[...]
    lane = lax.broadcasted_iota(jnp.int32, (xa_ref.shape[0], LANES), 1)
    for (s0, ns, rope, scale, ois) in plan:
        acc = _nn(xb, w_ref[:, s0 * LANES:(s0 + ns) * LANES]) + b_ref[:, s0 * LANES:(s0 + ns) * LANES]
        for j in range(ns):
            slab = acc[:, j * LANES:(j + 1) * LANES]
            if rope:
                slab = _rope_slab(slab, cos, sin, lane)
            if scale != 1.0:
                slab = slab * scale
            for oi in ois:
                o_ref = out_refs[oi]
                o_ref[:, j * LANES:(j + 1) * LANES] = slab.astype(o_ref.dtype)


def _project(x_parts, w, b, cos, sin, pos_block, plan, outs, tile):
    xa, xb, tiles_a, n = _split_parts(x_parts, tile)
    d = xa.shape[1]
    f = w.shape[1]
    out_shape = [jax.ShapeDtypeStruct((n, wd), dt) for wd, dt in outs]
    out_specs = [pl.BlockSpec((tile, wd), lambda i: (i, 0)) for wd, _ in outs]
    return pl.pallas_call(
        functools.partial(_proj_kernel, plan=plan, tiles_a=tiles_a),
        grid=(n // tile,),
        in_specs=_two_part_specs(tiles_a, tile, d) + [
                  pl.BlockSpec((d, f), lambda i: (0, 0)),
                  pl.BlockSpec((1, f), lambda i: (0, 0)),
                  pl.BlockSpec((tile, LANES), lambda i: (pos_block(i), 0)),
                  pl.BlockSpec((tile, LANES), lambda i: (pos_block(i), 0))],
        out_specs=out_specs,
        out_shape=out_shape,
        compiler_params=_cparams(("parallel",)),
        name="proj",
    )(xa, xb, w, b, cos, sin)


def _projT_kernel(x_ref, wt_ref, o_ref, *, scale):
    acc = _nt(wt_ref[...], x_ref[...].astype(BF16))
    if scale != 1.0:
        acc = acc * scale
    o_ref[...] = acc.astype(o_ref.dtype)


def _project_t(x, wt, scale, dtype, tile):
    n, d = x.shape
    f = wt.shape[0]
    return pl.pallas_call(
        functools.partial(_projT_kernel, scale=scale),
        grid=(n // tile,),
        in_specs=[pl.BlockSpec((tile, d), lambda i: (i, 0)),
                  pl.BlockSpec((f, d), lambda i: (0, 0))],
        out_specs=pl.BlockSpec((f, tile), lambda i: (0, i)),
        out_shape=jax.ShapeDtypeStruct((f, n), dtype),
        compiler_params=_cparams(("parallel",)),
        name="proj_t",
    )(x, wt)


def _swa_prompt_kernel(sink_ref, q_ref, kp_ref, kc_ref, vp_ref, vc_ref, o_ref):
    n = pl.program_id(1)
    kband = jnp.concatenate([kp_ref[...], kc_ref[...]], axis=0)
    vband = jnp.concatenate([vp_ref[...], vc_ref[...]], axis=0)
    qi = lax.broadcasted_iota(jnp.int32, (BLOCK, 2 * BLOCK), 0)
    kj = lax.broadcasted_iota(jnp.int32, (BLOCK, 2 * BLOCK), 1)
    rel = qi - kj + BLOCK
    valid = (rel >= 0) & (rel < WINDOW) & ((n > 0) | (kj >= BLOCK))
    lane = lax.broadcasted_iota(jnp.int32, (BLOCK, LANES), 1)
    low = lane < HEAD_DIM
    for pair in range(N_HEADS // 2):
        slab = q_ref[:, pair * LANES:(pair + 1) * LANES]
        outs = []
        for half in range(2):
            p = 2 * pair + half
            qe = jnp.where(low if half == 0 else ~low, slab, jnp.zeros_like(slab))
            s = jnp.where(valid, _nt(qe, kband), NEG_INF)
            sink = sink_ref[p]
            m = jnp.maximum(jnp.max(s, axis=1, keepdims=True), sink)
            pr = jnp.exp(s - m)
            den = jnp.sum(pr, axis=1, keepdims=True) + jnp.exp(sink - m)
            outs.append(_nn(pr.astype(BF16), vband) / den)
        o_ref[:, pair * LANES:(pair + 1) * LANES] = jnp.where(low, outs[0], outs[1]).astype(o_ref.dtype)


def _swa_prompt(q, kb, vb, sinks, batch, seq):
    nb = seq // BLOCK
    cur = lambda b, n, s: (b * nb + n, 0)
    prev = lambda b, n, s: (b * nb + jnp.maximum(n - 1, 0), 0)
    return pl.pallas_call(
        _swa_prompt_kernel,
        grid_spec=pltpu.PrefetchScalarGridSpec(
            num_scalar_prefetch=1, grid=(batch, nb),
            in_specs=[pl.BlockSpec((BLOCK, D_MODEL), cur),
                      pl.BlockSpec((BLOCK, LANES), prev), pl.BlockSpec((BLOCK, LANES), cur),
                      pl.BlockSpec((BLOCK, LANES), prev), pl.BlockSpec((BLOCK, LANES), cur)],
            out_specs=pl.BlockSpec((BLOCK, D_MODEL), cur)),
        out_shape=jax.ShapeDtypeStruct((batch * seq, D_MODEL), BF16),
        compiler_params=_cparams(("parallel", "parallel")),
        name="swa_prompt",
    )(sinks, q, kb, kb, vb, vb)


def _swa_sample_kernel(q_ref, kw_ref, kn_ref, vw_ref, vn_ref, sink_ref, o_ref, *, nb, dec_seq):
    w = kw_ref.shape[2]
    rows = q_ref.shape[1]
    lane = lax.broadcasted_iota(jnp.int32, (LANES, w), 0)
    lowf = lane < HEAD_DIM
    lane_n = lax.broadcasted_iota(jnp.int32, (LANES, LANES), 1)
    lown = lane_n < HEAD_DIM
    tok = lax.broadcasted_iota(jnp.int32, (rows, w), 0) % dec_seq
    col = lax.broadcasted_iota(jnp.int32, (rows, w), 1)
    valid_w = col > tok + (w - WINDOW)
    tok_n = lax.broadcasted_iota(jnp.int32, (rows, LANES), 0) % dec_seq
    col_n = lax.broadcasted_iota(jnp.int32, (rows, LANES), 1)
    valid_n = (col_n <= tok_n) & (col_n < dec_seq)
    lane_o = lax.broadcasted_iota(jnp.int32, (rows, LANES), 1)
    for i in range(nb):
        x = q_ref[i]
        kw = kw_ref[i].astype(BF16)
        vw = vw_ref[i].astype(BF16)
        kn = kn_ref[i].astype(BF16)
        vn = vn_ref[i].astype(BF16)
        outs = []
        for half in range(2):
            fm = lowf if half == 0 else ~lowf
            nm = lown if half == 0 else ~lown
            sw = jnp.where(valid_w, _nn(x, jnp.where(fm, kw, jnp.zeros_like(kw))), NEG_INF)
            sn = jnp.where(valid_n, _nt(x, jnp.where(nm, kn, jnp.zeros_like(kn))), NEG_INF)
            sink = sink_ref[:, half:half + 1]
            m = jnp.maximum(jnp.maximum(jnp.max(sw, axis=1, keepdims=True),
                                        jnp.max(sn, axis=1, keepdims=True)), sink)
            pw = jnp.exp(sw - m)
            pn = jnp.exp(sn - m)
            den = jnp.sum(pw, axis=1, keepdims=True) + jnp.sum(pn, axis=1, keepdims=True) + jnp.exp(sink - m)
            acc = _nt(pw.astype(BF16), vw) + _nn(pn.astype(BF16), vn)
            outs.append(acc / den)
        o_ref[i] = jnp.where(lane_o < HEAD_DIM, outs[0], outs[1]).astype(o_ref.dtype)


def _swa_sample(qx, kwt, knew, vwt, vnew, sinkmat, dec_seq, nb=8):
    bd, rows, _ = qx.shape
    w = kwt.shape[2]
    blk = lambda s: pl.BlockSpec((nb,) + s, lambda i: (i, 0, 0))
    return pl.pallas_call(
        functools.partial(_swa_sample_kernel, nb=nb, dec_seq=dec_seq),
        grid=(bd // nb,),
        in_specs=[blk((rows, LANES)), blk((LANES, w)), blk((LANES, LANES)), blk((LANES, w)), blk((LANES, LANES)),
                  pl.BlockSpec((rows, LANES), lambda i: (0, 0))],
        out_specs=blk((rows, LANES)),
        out_shape=jax.ShapeDtypeStruct((bd, rows, LANES), F32),
        compiler_params=_cparams(("parallel",)),
        name="swa_sample",
    )(qx, kwt, knew, vwt, vnew, sinkmat)


def _layer_norm(y, g, b):
    mu = jnp.mean(y, axis=1, keepdims=True)
    d = y - mu
    var = jnp.mean(d * d, axis=1, keepdims=True)
    return d * lax.rsqrt(var + LN_EPS) * g + b


ROUTE_E1, ROUTE_E2, ROUTE_G1, ROUTE_G2, ROUTE_R1, ROUTE_R2 = range(6)


def _out_ln_router_kernel(oa_ref, ob_ref, xa_ref, xb_ref, wo_ref, g_ref, b_ref, rh_ref, rl_ref, rb_ref, tri_ref,
                          y_ref, route_ref, route_t_ref, count_ref, run_ref, *, o_tiles_a, x_tiles_a):
    @pl.when(pl.program_id(0) == 0)
    def _():
        run_ref[...] = jnp.zeros_like(run_ref)

    o = _two_part_load(oa_ref, ob_ref, o_tiles_a)
    x = _two_part_load(xa_ref, xb_ref, x_tiles_a)
    y = _layer_norm(DEEPNORM_ALPHA * x + _nn(o, wo_ref[...]), g_ref[...], b_ref[...])
    y_ref[...] = y
    yh = y.astype(BF16)
    yl = (y - yh.astype(F32)).astype(BF16)
    logits = _nn(yh, rh_ref[...]) + (_nn(yl, rh_ref[...]) + _nn(yh, rl_ref[...])) + rb_ref[...]
    lane = lax.broadcasted_iota(jnp.int32, logits.shape, 1)
    is_g = (lane >= N_EXPERTS) & (lane < N_EXPERTS + N_GROUPS)
    gl = jnp.where(is_g, logits, NEG_INF)
    gmax = jnp.max(gl, axis=1, keepdims=True)
    gsel = jnp.min(jnp.where(gl == gmax, lane, 2 * LANES), axis=1, keepdims=True) - N_EXPERTS
    gprob = 1.0 / jnp.sum(jnp.exp(gl - gmax), axis=1, keepdims=True)
    in_g = (lane >= gsel * EXPERTS_PER_GROUP) & (lane < (gsel + 1) * EXPERTS_PER_GROUP)
    el = jnp.where(in_g, logits, NEG_INF)
    v1 = jnp.max(el, axis=1, keepdims=True)
    i1 = jnp.min(jnp.where(el == v1, lane, 2 * LANES), axis=1, keepdims=True)
    el2 = jnp.where(lane == i1, NEG_INF, el)
    v2 = jnp.max(el2, axis=1, keepdims=True)
    i2 = jnp.min(jnp.where(el2 == v2, lane, 2 * LANES), axis=1, keepdims=True)
    e2 = jnp.exp(v2 - v1)
    den = 1.0 + e2
    g1 = (1.0 / den) * gprob
    g2 = (e2 / den) * gprob
    hit1 = lane == i1
    hit2 = lane == i2
    oh = jnp.where(hit1 | hit2, 1.0, 0.0)
    pos = run_ref[...] + _nn(tri_ref[...], oh.astype(BF16))
    r1 = jnp.sum(jnp.where(hit1, pos, 0.0), axis=1, keepdims=True)
    r2 = jnp.sum(jnp.where(hit2, pos, 0.0), axis=1, keepdims=True)
    run_ref[...] = run_ref[...] + jnp.sum(oh, axis=0, keepdims=True)
    count_ref[...] = run_ref[...]
    rec = jnp.zeros(logits.shape, F32)
    for col, val in ((ROUTE_E1, i1.astype(F32)), (ROUTE_E2, i2.astype(F32)), (ROUTE_G1, g1), (ROUTE_G2, g2),
                     (ROUTE_R1, r1), (ROUTE_R2, r2)):
        rec = jnp.where(lane == col, val, rec)
    route_ref[...] = rec
    route_t_ref[...] = rec.T[0:8, :]


def _out_ln_router(o_parts, x_parts, wo, g, b, rh, rl, rb, tile):
    oa, ob, o_tiles_a, n = _split_parts(o_parts, tile)
    xa, xb, x_tiles_a, _ = _split_parts(x_parts, tile)
    row = lambda i: (i, 0)
    fix = lambda i: (0, 0)
    tri = jnp.tril(jnp.ones((tile, tile), BF16), -1)
    return pl.pallas_call(
        functools.partial(_out_ln_router_kernel, o_tiles_a=o_tiles_a, x_tiles_a=x_tiles_a),
        grid=(n // tile,),
        in_specs=_two_part_specs(o_tiles_a, tile, D_MODEL) + _two_part_specs(x_tiles_a, tile, D_MODEL) + [
                  pl.BlockSpec((D_MODEL, D_MODEL), fix), pl.BlockSpec((1, D_MODEL), fix),
                  pl.BlockSpec((1, D_MODEL), fix), pl.BlockSpec((D_MODEL, LANES), fix),
                  pl.BlockSpec((D_MODEL, LANES), fix), pl.BlockSpec((1, LANES), fix),
                  pl.BlockSpec((tile, tile), fix)],
        out_specs=[pl.BlockSpec((tile, D_MODEL), row), pl.BlockSpec((tile, LANES), row),
                   pl.BlockSpec((8, tile), lambda i: (0, i)), pl.BlockSpec((1, LANES), fix)],
        out_shape=[jax.ShapeDtypeStruct((n, D_MODEL), F32), jax.ShapeDtypeStruct((n, LANES), F32),
                   jax.ShapeDtypeStruct((8, n), F32), jax.ShapeDtypeStruct((1, LANES), F32)],
        scratch_shapes=[pltpu.VMEM((1, LANES), F32)],
        compiler_params=_cparams(("arbitrary",)),
        name="out_ln_router",
    )(oa, ob, xa, xb, wo, g, b, rh, rl, rb, tri)


MOE_ROW_TILE = 512


def _row_copy(src_ref, src_row, dst_ref, dst_row, sem):
    return pltpu.make_async_copy(src_ref.at[pl.ds(src_row, 1), :], dst_ref.at[pl.ds(dst_row, 1), :], sem)


def _moe_scatter_kernel(zt_ref, nu_ref, dest_ref, y_ref, xs_ref, zbuf, sem, zsem, *, tm, n_tiles):
    t = y_ref.shape[0]

    @pl.when(pl.program_id(0) == 0)
    def _():
        zbuf[...] = jnp.zeros_like(zbuf)

        def fill(row0):
            return pltpu.make_async_copy(zbuf, xs_ref.at[pl.ds(pl.multiple_of(row0, tm), tm), :], zsem)

        for wait in (False, True):
            for e in range(N_EXPERTS):
                @pl.when(zt_ref[e] >= 0)
                def _(e=e, wait=wait):
                    fill(zt_ref[e]).wait() if wait else fill(zt_ref[e]).start()

            def tail(k, c, wait=wait):
                fill(k * tm).wait() if wait else fill(k * tm).start()
                return c

            lax.fori_loop(nu_ref[0], n_tiles, tail, 0)

    def start_row(r, c):
        for k in range(2):
            _row_copy(y_ref, r, xs_ref, dest_ref[0, 0, k * t + r], sem).start(priority=k)
        return c

    def wait_row(r, c):
        for k in range(2):
            _row_copy(y_ref, 0, xs_ref, 0, sem).wait()
        return c

    lax.fori_loop(0, t, start_row, 0, unroll=8)
    lax.fori_loop(0, t, wait_row, 0, unroll=8)


def _moe_scatter(y, dest, ztile, n_used, n_tiles, tile):
    n = y.shape[0]
    tm = MOE_ROW_TILE
    return pl.pallas_call(
        functools.partial(_moe_scatter_kernel, tm=tm, n_tiles=n_tiles),
        grid_spec=pltpu.PrefetchScalarGridSpec(
            num_scalar_prefetch=2, grid=(n // tile,),
            in_specs=[pl.BlockSpec((1, 1, 2 * tile), lambda i, zt, nu: (i, 0, 0), memory_space=pltpu.SMEM),
                      pl.BlockSpec((tile, D_MODEL), lambda i, zt, nu: (i, 0))],
            out_specs=pl.BlockSpec(memory_space=pl.ANY),
            scratch_shapes=[pltpu.VMEM((tm, D_MODEL), F32), pltpu.SemaphoreType.DMA(()),
                            pltpu.SemaphoreType.DMA(())]),
        out_shape=jax.ShapeDtypeStruct((n_tiles * tm, D_MODEL), F32),
        compiler_params=_cparams(("arbitrary",)),
        name="moe_scatter",
    )(ztile, n_used, dest, y)


def _moe_experts_kernel(te_ref, nu_ref, x_ref, wgu_ref, wdn_ref, o_ref):
    i = pl.program_id(0)

    @pl.when(i < nu_ref[0])
    def _():
        gu = _nn(x_ref[...].astype(BF16), wgu_ref[0, 0].astype(BF16))
        gate, up = gu[:, :D_FF_EXPERT], gu[:, D_FF_EXPERT:]
        h = (gate / (1.0 + jnp.exp(-gate))) * up
        o_ref[...] = _nn(h.astype(BF16), wdn_ref[0, 0].astype(BF16))

    @pl.when(i >= nu_ref[0])
    def _():
        o_ref[...] = jnp.zeros_like(o_ref)


def _moe_experts(xs, wgu, wdn, layer, tile_expert, n_used):
    tm = MOE_ROW_TILE
    n_tiles = xs.shape[0] // tm
    used = lambda i, nu: jnp.minimum(i, nu[0] - 1)
    return pl.pallas_call(
        _moe_experts_kernel,
        grid_spec=pltpu.PrefetchScalarGridSpec(
            num_scalar_prefetch=2, grid=(n_tiles,),
            in_specs=[pl.BlockSpec((tm, D_MODEL), lambda i, te, nu: (used(i, nu), 0)),
                      pl.BlockSpec((1, 1, D_MODEL, 2 * D_FF_EXPERT),
                                   lambda i, te, nu: (layer, te[used(i, nu)], 0, 0)),
                      pl.BlockSpec((1, 1, D_FF_EXPERT, D_MODEL),
                                   lambda i, te, nu: (layer, te[used(i, nu)], 0, 0))],
            out_specs=pl.BlockSpec((tm, D_MODEL), lambda i, te, nu: (i, 0))),
        out_shape=jax.ShapeDtypeStruct(xs.shape, F32),
        compiler_params=_cparams(("arbitrary",)),
        name="moe_experts",
    )(tile_expert, n_used, xs, wgu, wdn)


def _moe_combine_kernel(dcur_ref, dnext_ref, x_ref, route_ref, g_ref, b_ref, ys_ref, *rest, tiles_a):
    *o_refs, buf, sem = rest
    i = pl.program_id(0)
    t = x_ref.shape[0]
    slot = i % 2

    def gather(dref, slot_, wait):
        def body(r, c):
            for k in range(2):
                src = 0 if wait else dref[0, 0, k * t + r]
                cp = pltpu.make_async_copy(ys_ref.at[pl.ds(src, 1), :],
                                           buf.at[slot_, k, pl.ds(r, 1), :], sem.at[slot_])
                cp.wait() if wait else cp.start(priority=k)
            return c
        lax.fori_loop(0, t, body, 0, unroll=8)

    @pl.when(i == 0)
    def _():
        gather(dcur_ref, 0, False)

    @pl.when(i + 1 < pl.num_programs(0))
    def _():
        gather(dnext_ref, 1 - slot, False)

    gather(dcur_ref, slot, True)
    route = route_ref[...]
    lane = lax.broadcasted_iota(jnp.int32, route.shape, 1)
    g1 = jnp.sum(jnp.where(lane == ROUTE_G1, route, 0.0), axis=1, keepdims=True)
    g2 = jnp.sum(jnp.where(lane == ROUTE_G2, route, 0.0), axis=1, keepdims=True)
    y = g1 * buf[slot, 0] + g2 * buf[slot, 1]
    out = _layer_norm(DEEPNORM_ALPHA * x_ref[...] + y, g_ref[...], b_ref[...])
    if len(o_refs) == 1:
        o_refs[0][...] = out
    else:
        @pl.when(i < tiles_a)
        def _():
            o_refs[0][...] = out

        @pl.when(i >= tiles_a)
        def _():
            o_refs[1][...] = out


def _moe_combine(x, route, dest, ys, g, b, tile, split_at=None):
    n = x.shape[0]
    nt = n // tile
    row = lambda i: (i, 0)
    fix = lambda i: (0, 0)
    if split_at is None:
        tiles_a = nt
        out_specs = pl.BlockSpec((tile, D_MODEL), row)
        out_shape = jax.ShapeDtypeStruct((n, D_MODEL), F32)
    else:
        tiles_a = split_at // tile
        out_specs = [pl.BlockSpec((tile, D_MODEL), lambda i: (jnp.minimum(i, tiles_a - 1), 0)),
                     pl.BlockSpec((tile, D_MODEL), lambda i: (jnp.maximum(i - tiles_a, 0), 0))]
        out_shape = [jax.ShapeDtypeStruct((split_at, D_MODEL), F32),
                     jax.ShapeDtypeStruct((n - split_at, D_MODEL), F32)]
    return pl.pallas_call(
        functools.partial(_moe_combine_kernel, tiles_a=tiles_a),
        grid_spec=pltpu.PrefetchScalarGridSpec(
            num_scalar_prefetch=0, grid=(nt,),
            in_specs=[pl.BlockSpec((1, 1, 2 * tile), lambda i: (i, 0, 0), memory_space=pltpu.SMEM),
                      pl.BlockSpec((1, 1, 2 * tile), lambda i: (jnp.minimum(i + 1, nt - 1), 0, 0),
                                   memory_space=pltpu.SMEM),
                      pl.BlockSpec((tile, D_MODEL), row), pl.BlockSpec((tile, LANES), row),
                      pl.BlockSpec((1, D_MODEL), fix), pl.BlockSpec((1, D_MODEL), fix),
                      pl.BlockSpec(memory_space=pl.ANY)],
            out_specs=out_specs,
            scratch_shapes=[pltpu.VMEM((2, 2, tile, D_MODEL), F32), pltpu.SemaphoreType.DMA((2,))]),
        out_shape=out_shape,
        compiler_params=_cparams(("arbitrary",)),
        name="moe_combine",
    )(dest, dest, x, route, g, b, ys)


def _moe_dest_kernel(off_ref, rt_ref, o_ref):
    rec = rt_ref[...].astype(jnp.int32)
    base = jnp.zeros_like(rec)
    for e in range(N_EXPERTS):
        base = jnp.where(rec == e, off_ref[e], base)
    o_ref[...] = base + pltpu.roll(rec, 8 - (ROUTE_R1 - ROUTE_E1), axis=0)


def _moe_dest(route_t, offs, tile):
    n = route_t.shape[1]
    return pl.pallas_call(
        _moe_dest_kernel,
        grid_spec=pltpu.PrefetchScalarGridSpec(
            num_scalar_prefetch=1, grid=(n // tile,),
            in_specs=[pl.BlockSpec((8, tile), lambda i, off: (0, i))],
            out_specs=pl.BlockSpec((8, tile), lambda i, off: (0, i))),
        out_shape=jax.ShapeDtypeStruct((8, n), jnp.int32),
        compiler_params=_cparams(("parallel",)),
        name="moe_dest",
    )(offs, route_t)


def _tile_dest(dest8, tile):
    return dest8[0:2].reshape(2, -1, tile).transpose(1, 0, 2).reshape(-1, 1, 2 * tile)


def _sparse_moe(y, route, route_t, counts, wgu, wdn, layer, g, b, tile, combine_tile, split_at=None):
    n = y.shape[0]
    tm = MOE_ROW_TILE
    n_tiles = -(-(2 * n + N_EXPERTS * (tm - 1)) // tm)
    cnt = counts[0, :N_EXPERTS].astype(jnp.int32)
    padded = ((cnt + tm - 1) // tm) * tm
    ends = jnp.cumsum(padded)
    offs = ends - padded
    n_used = (ends[-1] // tm).reshape(1)
    tile_row0 = jnp.arange(n_tiles, dtype=jnp.int32) * tm
    tile_expert = jnp.minimum(jnp.sum(tile_row0[:, None] >= ends[None, :], axis=1), N_EXPERTS - 1).astype(jnp.int32)
    ztile = jnp.where(padded > 0, ends - tm, -1).astype(jnp.int32)
    dest8 = _moe_dest(route_t, offs, tile)
    xs = _moe_scatter(y, _tile_dest(dest8, tile), ztile, n_used, n_tiles, tile)
    ys = _moe_experts(xs, wgu, wdn, layer, tile_expert, n_used)
    return _moe_combine(y, route, _tile_dest(dest8, combine_tile), ys, g, b, combine_tile, split_at)


def _reduce0(x, op):
    r, w = x.shape
    g = next(g for g in (64, 32, 16, 8) if r % g == 0)
    return op(op(x.reshape(r // g, g, w), axis=0), axis=0, keepdims=True)


def _count(mask, axis):
    ones = jnp.where(mask, 1.0, 0.0)
    if axis == 0:
        return _reduce0(ones, jnp.sum)
    return jnp.sum(ones, axis=axis, keepdims=True)


def _order_key(x):
    bits = lax.bitcast_convert_type(x, jnp.int32)
    return jnp.where(x == 0.0, 0, jnp.where(bits < 0, bits ^ 0x7FFFFFFF, bits))


def _topk_mask(load_key, idx, k, axis, idx_bits):
    kf = float(k)
    r = jnp.where(_count(load_key() >= 0, axis) >= kf, 0, INT_MIN).astype(jnp.int32)

    def value_bit(i, r):
        cand = r | jnp.left_shift(jnp.int32(1), 30 - i)
        return jnp.where(_count(load_key() >= cand, axis) >= kf, cand, r)

    r = lax.fori_loop(0, 31, value_bit, r)
    need = kf - _count(load_key() > r, axis)
    n_tie = _count(load_key() == r, axis)

    def index_bit(i, j):
        cand = j | jnp.left_shift(jnp.int32(1), idx_bits - 1 - i)
        return jnp.where(_count((load_key() == r) & (idx < cand), axis) < need, cand, j)

    some_left_out = jnp.max(jnp.where(need < n_tie, 1.0, 0.0)) > 0.0
    j = lax.cond(some_left_out,
                 lambda: lax.fori_loop(0, idx_bits, index_bit, jnp.zeros_like(r)),
                 lambda: jnp.full_like(r, (1 << idx_bits) - 1))
    key = load_key()
    return (key > r) | ((key == r) & (idx <= j))


def _dsa_prompt_kernel(q_ref, kb_ref, vb_ref, kid_ref, qi_ref, wit_ref, o_ref, key_ref,
                       *, classes, topk, slot_kv):
    n = pl.program_id(1)
    for n0, cnt in classes:
        @pl.when((n >= n0) & (n < n0 + cnt))
        def _(n0=n0, cnt=cnt):
            _dsa_prompt_block(q_ref, kb_ref, vb_ref, kid_ref, qi_ref, wit_ref, o_ref, key_ref,
                              sk=(n0 + cnt) * BLOCK, topk=topk, slot_kv=slot_kv)


def _dsa_prompt_block(q_ref, kb_ref, vb_ref, kid_ref, qi_ref, wit_ref, o_ref, key_ref, *, sk, topk, slot_kv):
    t0 = pl.program_id(1) * BLOCK
    lane = lax.broadcasted_iota(jnp.int32, (BLOCK, LANES), 1)
    low = lane < HEAD_DIM
    s_idx = lax.broadcasted_iota(jnp.int32, (sk, BLOCK), 0)
    causal = s_idx <= t0 + lax.broadcasted_iota(jnp.int32, (sk, BLOCK), 1)

    kid = kid_ref[0:sk, :]
    sc = jnp.zeros((sk, BLOCK), F32)
    for hp in range(IDX_HEADS // 2):
        slab = qi_ref[:, hp * LANES:(hp + 1) * LANES]
        z = jnp.zeros_like(slab)
        qh2 = jnp.concatenate([jnp.where(low, slab, z), jnp.where(low, z, slab)], axis=0)
        d = jnp.maximum(_nt(kid, qh2), 0.0)
        sc = sc + d[:, :BLOCK] * wit_ref[2 * hp:2 * hp + 1, :] + d[:, BLOCK:] * wit_ref[2 * hp + 1:2 * hp + 2, :]
    key_ref[0:sk, :] = _order_key(jnp.where(causal, sc, NEG_INF))
    sel = _topk_mask(lambda: key_ref[0:sk, :], s_idx, topk, 0, (sk - 1).bit_length()) & causal
    bias = jnp.where(sel, 0.0, NEG_INF).T

    kb = kb_ref[0:sk, :]
    vb = vb_ref[0:sk, :]
    res = [None] * N_HEADS
    for kv in range(KV_HEADS_B):
        slots = [p for p in range(N_HEADS) if slot_kv[p] == kv]
        qes = []
        for p in slots:
            slab = q_ref[:, (p // 2) * LANES:(p // 2 + 1) * LANES]
            z = jnp.zeros_like(slab)
            qm = jnp.where(low, slab, z) if p % 2 == 0 else jnp.where(low, z, slab)
            qes.append(jnp.concatenate([qm, z] if kv // 2 == 0 else [z, qm], axis=1))
        s_all = _nt(jnp.concatenate(qes, axis=0), kb)
        prs, dens = [], []
        for i in range(len(slots)):
            s = s_all[i * BLOCK:(i + 1) * BLOCK, :] + bias
            pr = jnp.exp2(s - jnp.max(s, axis=1, keepdims=True))
            dens.append(jnp.sum(pr, axis=1, keepdims=True))
            prs.append(pr.astype(BF16))
        o_all = _nn(jnp.concatenate(prs, axis=0), vb)
        for i, p in enumerate(slots):
            res[p] = o_all[i * BLOCK:(i + 1) * BLOCK, (kv // 2) * LANES:(kv // 2 + 1) * LANES] / dens[i]
    for pair in range(N_HEADS // 2):
        o_ref[:, pair * LANES:(pair + 1) * LANES] = jnp.where(low, res[2 * pair], res[2 * pair + 1]).astype(o_ref.dtype)


def _dsa_prompt(q, kb, vb, kid, qi, wit, batch, seq, topk, slot_kv, n_classes=4):
    nb = seq // BLOCK
    per = -(-nb // n_classes)
    classes = tuple((n0, min(per, nb - n0)) for n0 in range(0, nb, per))
    qmap = lambda b, n: (b * nb + n, 0)
    return pl.pallas_call(
        functools.partial(_dsa_prompt_kernel, classes=classes, topk=topk, slot_kv=slot_kv),
        grid=(batch, nb),
        in_specs=[pl.BlockSpec((BLOCK, D_MODEL), qmap),
                  pl.BlockSpec((seq, 2 * LANES), lambda b, n: (b, 0)),
                  pl.BlockSpec((seq, 2 * LANES), lambda b, n: (b, 0)),
                  pl.BlockSpec((seq, LANES), lambda b, n: (b, 0)),
                  pl.BlockSpec((BLOCK, IDX_HEADS * IDX_DIM), qmap),
                  pl.BlockSpec((IDX_HEADS, BLOCK), lambda b, n: (0, b * nb + n))],
        out_specs=pl.BlockSpec((BLOCK, D_MODEL), qmap),
        out_shape=jax.ShapeDtypeStruct((batch * seq, D_MODEL), BF16),
        scratch_shapes=[pltpu.VMEM((seq, BLOCK), jnp.int32)],
        compiler_params=_cparams(("parallel", "arbitrary")),
        name="dsa_prompt",
    )(q, kb, vb, kid, qi, wit)


def _page_copies(pt_ref, pool_ref, buf_ref, sem, step, slot, pairb, page0, n_pages):
    cps = []
    for bi in range(pairb):
        for i in range(n_pages):
            page = pt_ref[step * pairb + bi, page0 + i]
            cps.append(pltpu.make_async_copy(
                pool_ref.at[page], buf_ref.at[slot, bi, :, pl.ds(i * PAGE_SIZE, PAGE_SIZE)], sem.at[slot]))
    return cps


def _dsa_sample_scores_kernel(pt_ref, pool_ref, qi_ref, w_ref, kin_ref, o_ref, buf_ref, sem, *, pairb, n_pages, dec_seq):
    g = pl.program_id(0)
    ng = pl.num_programs(0)
    slot = g % 2

    @pl.when(g == 0)
    def _():
        for cp in _page_copies(pt_ref, pool_ref, buf_ref, sem, 0, 0, pairb, 0, n_pages):
            cp.start()

    @pl.when(g + 1 < ng)
    def _():
        for cp in _page_copies(pt_ref, pool_ref, buf_ref, sem, g + 1, 1 - slot, pairb, 0, n_pages):
            cp.start()

    for cp in _page_copies(pt_ref, pool_ref, buf_ref, sem, g, slot, pairb, 0, n_pages):
        cp.wait()

    past = n_pages * PAGE_SIZE
    lane = lax.broadcasted_iota(jnp.int32, (1, LANES), 1)
    for bi in range(pairb):
        qi = qi_ref[bi]
        w = w_ref[bi]
        r = jnp.maximum(_nn(qi, buf_ref[slot, bi].astype(BF16)), 0.0) * w
        rn = jnp.maximum(_nt(qi, kin_ref[bi]), 0.0) * w
        for t in range(dec_seq):
            row = bi * dec_seq + t
            o_ref[row:row + 1, 0:past] = jnp.sum(r[t * IDX_HEADS:(t + 1) * IDX_HEADS], axis=0, keepdims=True)
            new = jnp.sum(rn[t * IDX_HEADS:(t + 1) * IDX_HEADS], axis=0, keepdims=True)
            o_ref[row:row + 1, past:past + LANES] = jnp.where(lane <= t, new, NEG_INF)


def _dsa_sample_scores(page_table, pool_t, qix, wcol, kin, dec_seq, pairb=2):
    bd, n_pages = page_table.shape
    past = n_pages * PAGE_SIZE
    rows = dec_seq * IDX_HEADS
    blk = lambda s: pl.BlockSpec((pairb,) + s, lambda g, pt: (g, 0, 0))
    return pl.pallas_call(
        functools.partial(_dsa_sample_scores_kernel, pairb=pairb, n_pages=n_pages, dec_seq=dec_seq),
        grid_spec=pltpu.PrefetchScalarGridSpec(
            num_scalar_prefetch=1, grid=(bd // pairb,),
            in_specs=[pl.BlockSpec(memory_space=pl.ANY), blk((rows, IDX_DIM)), blk((rows, 1)), blk((LANES, IDX_DIM))],
            out_specs=pl.BlockSpec((pairb * dec_seq, past + LANES), lambda g, pt: (g, 0)),
            scratch_shapes=[pltpu.VMEM((2, pairb, IDX_DIM, past), F32), pltpu.SemaphoreType.DMA((2,))]),
        out_shape=jax.ShapeDtypeStruct((bd * dec_seq, past + LANES), F32),
        compiler_params=_cparams(("arbitrary",)),
        name="dsa_sample_scores",
    )(page_table, pool_t, qix, wcol, kin)


def _select_bias_kernel(sc_ref, o_ref, key_ref, *, topk):
    idx = lax.broadcasted_iota(jnp.int32, sc_ref.shape, 1)
    key_ref[...] = _order_key(sc_ref[...])
    sel = _topk_mask(lambda: key_ref[...], idx, topk, 1, (sc_ref.shape[1] - 1).bit_length())
    o_ref[...] = jnp.where(sel & (sc_ref[...] > NEG_INF), 0.0, NEG_INF)


def _select_bias(sc, topk, tile):
    r, c = sc.shape
    return pl.pallas_call(
        functools.partial(_select_bias_kernel, topk=topk),
        grid=(r // tile,),
        in_specs=[pl.BlockSpec((tile, c), lambda i: (i, 0))],
        out_specs=pl.BlockSpec((tile, c), lambda i: (i, 0)),
        out_shape=jax.ShapeDtypeStruct((r, c), F32),
        scratch_shapes=[pltpu.VMEM((tile, c), jnp.int32)],
        compiler_params=_cparams(("parallel",)),
        name="select_bias",
    )(sc)


def _dsa_sample_attn_kernel(pt_ref, kpool_ref, vpool_ref, qe_ref, bias_ref, biasn_ref, kn_ref, vn_ref, o_ref,
                            kbuf, vbuf, ksem, vsem, m_ref, l_ref, acc_ref, *, pairb, chunk_pages, dec_seq):
    g = pl.program_id(0)
    c = pl.program_id(1)
    nch = pl.num_programs(1)
    lin = g * nch + c
    total = pl.num_programs(0) * nch
    slot = lin % 2

    def copies(step_lin, slot_):
        gg = step_lin // nch
        cc = step_lin % nch
        return (_page_copies(pt_ref, kpool_ref, kbuf, ksem, gg, slot_, pairb, cc * chunk_pages, chunk_pages)
                + _page_copies(pt_ref, vpool_ref, vbuf, vsem, gg, slot_, pairb, cc * chunk_pages, chunk_pages))

    @pl.when(lin == 0)
    def _():
        for cp in copies(0, 0):
            cp.start()

    @pl.when(lin + 1 < total)
    def _():
        for cp in copies(lin + 1, 1 - slot):
            cp.start()

    for cp in copies(lin, slot):
        cp.wait()

    @pl.when(c == 0)
    def _():
        m_ref[...] = jnp.full_like(m_ref, -1e30)
        l_ref[...] = jnp.zeros_like(l_ref)
        acc_ref[...] = jnp.zeros_like(acc_ref)

    rows = qe_ref.shape[1]
    per_tok = rows // dec_seq

    def expand(b4):
        return jnp.concatenate(
            [jnp.broadcast_to(b4[t:t + 1, :], (per_tok, b4.shape[1])) for t in range(dec_seq)], axis=0)

    def update(bi, s, pv):
        m_old = m_ref[bi]
        m_new = jnp.maximum(m_old, jnp.max(s, axis=1, keepdims=True))
        alpha = jnp.exp2(m_old - m_new)
        pr = jnp.exp2(s - m_new)
        l_ref[bi] = alpha * l_ref[bi] + jnp.sum(pr, axis=1, keepdims=True)
        acc_ref[bi] = alpha * acc_ref[bi] + pv(pr.astype(BF16))
        m_ref[bi] = m_new

    for bi in range(pairb):
        qe = qe_ref[bi]
        kt = kbuf[slot, bi].astype(BF16)
        vt = vbuf[slot, bi].astype(BF16)
        s = _nn(qe, kt) + expand(bias_ref[bi * dec_seq:(bi + 1) * dec_seq, :])
        update(bi, s, lambda pr: _nt(pr, vt))

    @pl.when(c == nch - 1)
    def _():
        for bi in range(pairb):
            qe = qe_ref[bi]
            s = _nt(qe, kn_ref[bi]) + expand(biasn_ref[bi * dec_seq:(bi + 1) * dec_seq, :])
            update(bi, s, lambda pr: _nn(pr, vn_ref[bi]))
            o_ref[bi] = acc_ref[bi] / l_ref[bi]


def _dsa_sample_attn(page_table, kpool_t, vpool_t, qe, bias, knew, vnew, dec_seq, pairb=2, chunk_pages=16):
    bd, n_pages = page_table.shape
    nch = n_pages // chunk_pages
    chunk = chunk_pages * PAGE_SIZE
    rows = qe.shape[1]
    kvw = qe.shape[2]
    blk = lambda s: pl.BlockSpec((pairb,) + s, lambda g, c, pt: (g, 0, 0))
    return pl.pallas_call(
        functools.partial(_dsa_sample_attn_kernel, pairb=pairb, chunk_pages=chunk_pages, dec_seq=dec_seq),
        grid_spec=pltpu.PrefetchScalarGridSpec(
            num_scalar_prefetch=1, grid=(bd // pairb, nch),
            in_specs=[pl.BlockSpec(memory_space=pl.ANY), pl.BlockSpec(memory_space=pl.ANY),
                      blk((rows, kvw)),
                      pl.BlockSpec((pairb * dec_seq, chunk), lambda g, c, pt: (g, c)),
                      pl.BlockSpec((pairb * dec_seq, LANES), lambda g, c, pt: (g, n_pages)),
                      blk((LANES, kvw)), blk((LANES, kvw))],
            out_specs=blk((rows, kvw)),
            scratch_shapes=[pltpu.VMEM((2, pairb, kvw, chunk), F32), pltpu.VMEM((2, pairb, kvw, chunk), F32),
                            pltpu.SemaphoreType.DMA((2,)), pltpu.SemaphoreType.DMA((2,)),
                            pltpu.VMEM((pairb, rows, 1), F32), pltpu.VMEM((pairb, rows, 1), F32),
                            pltpu.VMEM((pairb, rows, kvw), F32)]),
        out_shape=jax.ShapeDtypeStruct((bd, rows, kvw), F32),
        compiler_params=_cparams(("arbitrary", "arbitrary")),
        name="dsa_sample_attn",
    )(page_table, kpool_t, vpool_t, qe, bias, bias, knew, vnew)


def _pick_tile(n, candidates):
    for t in candidates:
        if n % t == 0:
            return t
    raise ValueError(f"no token tile for {n}")


def _hi_lo(w):
    hi = w.astype(BF16)
    return hi, (w - hi.astype(F32)).astype(BF16)


def _pad_rows(x, rows):
    return jnp.pad(x, ((0, 0), (0, rows - x.shape[1]), (0, 0)))


def _post_attention(o_parts, x_parts, w_o_phys, l, moe, ln, tile, moe_tile, split_at=None):
    (w_rg, b_rg, w_re, b_re, w_gu, w_dn) = moe
    (ln1_g, ln1_b, ln2_g, ln2_b) = ln
    rw = jnp.zeros((D_MODEL, LANES), F32).at[:, :N_EXPERTS].set(w_re[l])
    rw = rw.at[:, N_EXPERTS:N_EXPERTS + N_GROUPS].set(w_rg[l])
    rb = jnp.zeros((1, LANES), F32).at[0, :N_EXPERTS].set(b_re[l]).at[0, N_EXPERTS:N_EXPERTS + N_GROUPS].set(b_rg[l])
    rh, rl = _hi_lo(rw)
    y, route, route_t, counts = _out_ln_router(o_parts, x_parts, w_o_phys.astype(BF16), ln1_g[l].reshape(1, -1),
                                               ln1_b[l].reshape(1, -1), rh, rl, rb, tile)
    return _sparse_moe(y, route, route_t, counts, w_gu, w_dn, l, ln2_g[l].reshape(1, -1),
                       ln2_b[l].reshape(1, -1), tile, moe_tile, split_at)


def kernel(x_prompt, x_sample, cache_win_k, cache_win_v, cache_k, cache_v, cache_idx_k, page_table,
           a_w_qkv, a_b_qkv, a_sinks, a_w_o, b_w_in, b_w_o,
           moe_w_rg, moe_b_rg, moe_w_re, moe_b_re, moe_w_gu, moe_w_dn,
           ln1_g, ln1_b, ln2_g, ln2_b):
    B, S, _ = x_prompt.shape
    Bd, T, _ = x_sample.shape
    n_p, n_s = B * S, Bd * T
    n_tot = n_p + n_s
    past = page_table.shape[1] * PAGE_SIZE
    tile = _pick_tile(math.gcd(S, n_s), (512, 256, 128))
    moe_tile = _pick_tile(tile, (256, 128))
    qd = N_HEADS * HEAD_DIM
    moe = (moe_w_rg, moe_b_rg, moe_w_re, moe_b_re, moe_w_gu, moe_w_dn)
    ln = (ln1_g, ln1_b, ln2_g, ln2_b)

    x_parts = (x_prompt.reshape(n_p, D_MODEL), x_sample.reshape(n_s, D_MODEL))
    pos = jnp.concatenate([jnp.arange(S, dtype=jnp.int32), past + jnp.arange(tile, dtype=jnp.int32) % T])
    cos, sin = _rope_tables(pos)
    pos_block = lambda i: jnp.where(i < n_p // tile, i % (S // tile), S // tile)
    scale = HEAD_DIM ** -0.5

    a = 0
    kva = KV_HEADS_A * HEAD_DIM
    order_a = _slot_order(KV_HEADS_A)
    perm_a = _col_perm(order_a)
    wq, wk, wv = a_w_qkv[a][:, :qd], a_w_qkv[a][:, qd:qd + kva], a_w_qkv[a][:, qd + kva:]
    bq, bk, bv = a_b_qkv[a][:qd], a_b_qkv[a][qd:qd + kva], a_b_qkv[a][qd + kva:]
    w_a = jnp.concatenate([wq[:, perm_a], wk, wv], axis=1).astype(BF16)
    b_a = jnp.concatenate([bq[perm_a], bk, bv]).reshape(1, -1)
    plan_a = ((0, 8, True, scale, (0,)), (8, 1, True, 1.0, (1, 2)), (9, 1, False, 1.0, (3, 4)))
    q0, k0, k0b, v0, v0b = _project(x_parts, w_a, b_a, cos, sin, pos_block, plan_a,
                                    [(qd, BF16), (kva, F32), (kva, BF16), (kva, F32), (kva, BF16)], tile)
    sinks_phys = a_sinks[a][np.asarray(order_a)]
    o_p = _swa_prompt(q0, k0b, v0b, sinks_phys, B, S)

    w_win = cache_win_k.shape[2]
    qx = q0[n_p:].reshape(Bd, T, N_HEADS // 2, LANES).transpose(0, 2, 1, 3).reshape(Bd, T * N_HEADS // 2, LANES)
    kwt = cache_win_k[a].transpose(0, 2, 3, 1).reshape(Bd, kva, w_win)
    vwt = cache_win_v[a].transpose(0, 2, 3, 1).reshape(Bd, kva, w_win)
    k0s = k0[n_p:].reshape(Bd, T, kva)
    v0s = v0[n_p:].reshape(Bd, T, kva)
    sinkmat = jnp.zeros((T * N_HEADS // 2, LANES), F32).at[:, :2].set(
        jnp.repeat(sinks_phys.reshape(N_HEADS // 2, 2), T, axis=0))
    ox = _swa_sample(qx, kwt, _pad_rows(k0s, LANES), vwt, _pad_rows(v0s, LANES), sinkmat, T,
                     nb=_pick_tile(Bd, (8, 4, 2, 1)))
    o_s = ox.reshape(Bd, N_HEADS // 2, T, LANES).transpose(0, 2, 1, 3).reshape(n_s, qd).astype(BF16)

    x = _post_attention((o_p, o_s), x_parts, a_w_o[a][perm_a, :], 0, moe, ln, tile, moe_tile)

    wp = min(WINDOW, S)
    win_k_prompt = k0[:n_p].reshape(B, S, KV_HEADS_A, HEAD_DIM)[:, S - wp:][None]
    win_v_prompt = v0[:n_p].reshape(B, S, KV_HEADS_A, HEAD_DIM)[:, S - wp:][None]
    win_k_sample = jnp.concatenate([cache_win_k[a], k0s.reshape(Bd, T, KV_HEADS_A, HEAD_DIM)], axis=1)[:, -w_win:][None]
    win_v_sample = jnp.concatenate([cache_win_v[a], v0s.reshape(Bd, T, KV_HEADS_A, HEAD_DIM)], axis=1)[:, -w_win:][None]

    bl = 0
    kvb = KV_HEADS_B * HEAD_DIM
    qid = IDX_HEADS * IDX_DIM
    order_b = _slot_order(KV_HEADS_B)
    perm_b = _col_perm(order_b)
    slot_kv = tuple(h // (N_HEADS // KV_HEADS_B) for h in order_b)
    w_in = b_w_in[bl]
    c0 = 0
    wq = w_in[:, c0:c0 + qd]; c0 += qd
    wk = w_in[:, c0:c0 + kvb]; c0 += kvb
    wv = w_in[:, c0:c0 + kvb]; c0 += kvb
    wqi = w_in[:, c0:c0 + qid]; c0 += qid
    wki = w_in[:, c0:c0 + IDX_DIM]; c0 += IDX_DIM
    wwi = w_in[:, c0:c0 + IDX_HEADS]
    w_b = jnp.concatenate([wq[:, perm_b], wk, wv, wqi, wki, wki], axis=1).astype(BF16)
    b_b = jnp.zeros((1, w_b.shape[1]), F32)
    plan_b = ((0, 8, True, scale * math.log2(math.e), (0,)), (8, 2, True, 1.0, (1, 2)), (10, 2, False, 1.0, (3, 7)),
              (12, 4, True, IDX_DIM ** -0.5, (4,)), (16, 1, True, 1.0, (5, 6)))
    q1, k1, k1b, v1, qi1, ki1, kid1, v1b = _project(
        (x,), w_b, b_b, cos, sin, pos_block, plan_b,
        [(qd, BF16), (kvb, F32), (kvb, BF16), (kvb, F32), (qid, BF16), (LANES, F32), (LANES, BF16), (kvb, BF16)],
        tile)
    wwi_t = jnp.zeros((2 * IDX_HEADS, D_MODEL), F32).at[:IDX_HEADS].set(wwi.T).astype(BF16)
    wit1 = _project_t(x, wwi_t, IDX_HEADS ** -0.5, F32, tile)

    o_p = _dsa_prompt(q1, k1b, v1b, kid1, qi1, wit1, B, S, min(TOPK_MAX, S // 4), slot_kv)

    n_pool = cache_k.shape[1]
    pool_ik_t = cache_idx_k[bl].transpose(0, 2, 1)
    qix = qi1[n_p:].reshape(Bd, T * IDX_HEADS, IDX_DIM)
    wcol = wit1[:IDX_HEADS, n_p:].T.reshape(Bd, T * IDX_HEADS, 1)
    kin = _pad_rows(kid1[n_p:, :IDX_DIM].reshape(Bd, T, IDX_DIM), LANES)
    sc = _dsa_sample_scores(page_table, pool_ik_t, qix, wcol, kin, T)
    bias = _select_bias(sc, min(TOPK_MAX, (past + T) // 4), _pick_tile(n_s, (128, 64, 32, 16, 8)))
    kpool_t = cache_k[bl].transpose(0, 2, 3, 1).reshape(n_pool, kvb, PAGE_SIZE)
    vpool_t = cache_v[bl].transpose(0, 2, 3, 1).reshape(n_pool, kvb, PAGE_SIZE)
    onehot = (np.asarray(slot_kv)[:, None] == np.arange(KV_HEADS_B)[None, :]).astype(np.float32)
    qe = (q1[n_p:].reshape(n_s, N_HEADS, 1, HEAD_DIM) * jnp.asarray(onehot, BF16)[None, :, :, None])
    qe = qe.reshape(Bd, T * N_HEADS, kvb)
    knew = _pad_rows(k1b[n_p:].reshape(Bd, T, kvb), LANES)
    vnew = _pad_rows(v1b[n_p:].reshape(Bd, T, kvb), LANES)
    ox = _dsa_sample_attn(page_table, kpool_t, vpool_t, qe, bias, knew, vnew, T,
                          chunk_pages=_pick_tile(page_table.shape[1], (16, 8, 4, 2, 1)))
    ox = ox.reshape(Bd, T, N_HEADS, KV_HEADS_B, HEAD_DIM)
    o_s = ox[:, :, np.arange(N_HEADS), np.asarray(slot_kv), :].reshape(n_s, qd).astype(BF16)

    y_p, y_s = _post_attention((o_p, o_s), (x,), b_w_o[bl][perm_b, :], 1, moe, ln, tile, moe_tile, split_at=n_p)

    k_prompt = k1[:n_p].reshape(B, S, KV_HEADS_B, HEAD_DIM)[None]
    v_prompt = v1[:n_p].reshape(B, S, KV_HEADS_B, HEAD_DIM)[None]
    idx_k_prompt = ki1[:n_p, :IDX_DIM].reshape(B, S, IDX_DIM)[None]
    k_sample = k1[n_p:].reshape(Bd, T, KV_HEADS_B, HEAD_DIM)[None]
    v_sample = v1[n_p:].reshape(Bd, T, KV_HEADS_B, HEAD_DIM)[None]
    idx_k_sample = ki1[n_p:, :IDX_DIM].reshape(Bd, T, IDX_DIM)[None]

    return (y_p.reshape(B, S, D_MODEL), y_s.reshape(Bd, T, D_MODEL),
            win_k_prompt, win_v_prompt, win_k_sample, win_v_sample,
            k_prompt, v_prompt, idx_k_prompt, k_sample, v_sample, idx_k_sample)
```
